```python
import jax, jax.numpy as jnp
from jax import lax
import numpy as np

D_MODEL = 1024
BATCH = 8
SEQ = 2048
DEPTH = 4

CHUNK = 64
N_A = max(1, DEPTH // 2)
N_B = DEPTH - N_A
GDN_HEADS = 8
GDN_HEAD_DIM = 128
GDN_WIDTH = GDN_HEADS * GDN_HEAD_DIM
CONV_WIDTH = 4
SB_HEADS = 16
SB_HEAD_DIM = 64
SB_WIDTH = SB_HEADS * SB_HEAD_DIM
Q_BLOCK = 128
EPS = 1e-6

kernel_name = "yoco_gdn_stickbreaking_trunk"


def _rms(x, gain):
    xf = x.astype(jnp.float32)
    return xf * lax.rsqrt(jnp.mean(xf * xf, axis=-1, keepdims=True) + EPS) * gain.astype(jnp.float32)


def _l2norm(x):
    return x * lax.rsqrt(jnp.sum(x * x, axis=-1, keepdims=True) + EPS)


def _modulated_norm(x, gain, shift, scale):
    n = _rms(x, gain) * (1.0 + scale[:, None, :].astype(jnp.float32)) + shift[:, None, :].astype(jnp.float32)
    return n.astype(x.dtype)


def _causal_conv(x, w):
    C = x.shape[-1]
    return lax.conv_general_dilated(
        x, w[:, None, :].astype(x.dtype), window_strides=(1,),
        padding=[(CONV_WIDTH - 1, 0)], dimension_numbers=("NWC", "WIO", "NWC"),
        feature_group_count=C)


def _chunk_gated_delta(q, k, v, g, beta):
    B, T, H, Dk = q.shape
    Dv = v.shape[-1]
    N = T // CHUNK
    f32 = jnp.float32
    to_c = lambda a: a.astype(f32).reshape(B, N, CHUNK, H, a.shape[-1]).transpose(0, 3, 1, 2, 4)
    q, k, v = to_c(q), to_c(k), to_c(v)
    g = g.astype(f32).reshape(B, N, CHUNK, H).transpose(0, 3, 1, 2)
    beta = beta.astype(f32).reshape(B, N, CHUNK, H).transpose(0, 3, 1, 2)
    g = jnp.cumsum(g, axis=-1)
    k_beta = k * beta[..., None]
    v_beta = v * beta[..., None]
    tril = jnp.tril(jnp.ones((CHUNK, CHUNK), dtype=bool))
    strict = jnp.tril(jnp.ones((CHUNK, CHUNK), dtype=bool), -1)
    diff = g[..., :, None] - g[..., None, :]
    decay = jnp.where(tril, jnp.exp(jnp.where(tril, diff, 0.0)), 0.0)
    L = jnp.where(strict, jnp.einsum("bhncd,bhnsd->bhncs", k_beta, k) * decay, 0.0)
    a_mat = L + jnp.eye(CHUNK, dtype=f32)
    rhs = jnp.concatenate([v_beta, k_beta * jnp.exp(g)[..., None]], axis=-1)
    sol = lax.linalg.triangular_solve(a_mat, rhs, left_side=True, lower=True, unit_diagonal=True)
    u, w = sol[..., :Dv], sol[..., Dv:]
    attn = jnp.where(tril, jnp.einsum("bhncd,bhnsd->bhncs", q, k) * decay, 0.0)
    q_dec = q * jnp.exp(g)[..., None]
    k_dec = k * jnp.exp(g[..., -1:] - g)[..., None]
    g_last = jnp.exp(g[..., -1])

    def step(S, inp):
        q_i, w_i, u_i, a_i, k_i, gl_i = inp
        v_new = u_i - jnp.einsum("bhck,bhkv->bhcv", w_i, S)
        o = jnp.einsum("bhck,bhkv->bhcv", q_i, S) + jnp.einsum("bhcs,bhsv->bhcv", a_i, v_new)
        S = S * gl_i[..., None, None] + jnp.einsum("bhck,bhcv->bhkv", k_i, v_new)
        return S, o

    mv = lambda a: jnp.moveaxis(a, 2, 0)
    S0 = jnp.zeros((B, H, Dk, Dv), f32)
    _, o = lax.scan(step, S0, (mv(q_dec), mv(w), mv(u), mv(attn), mv(k_dec), mv(g_last)))
    return o.transpose(1, 0, 3, 2, 4).reshape(B, T, H, Dv)


def _gdn_mixer(h, w_in, conv_w, a_log, dt_bias, o_gain, w_out):
    B, T, _ = h.shape
    W, H, Dh = GDN_WIDTH, GDN_HEADS, GDN_HEAD_DIM
    proj = h @ w_in
    qkv = jax.nn.silu(_causal_conv(proj[..., :3 * W], conv_w)).astype(jnp.float32)
    z = proj[..., 3 * W:4 * W].astype(jnp.float32).reshape(B, T, H, Dh)
    b_raw = proj[..., 4 * W:4 * W + H].astype(jnp.float32)
    a_raw = proj[..., 4 * W + H:].astype(jnp.float32)
    q = _l2norm(qkv[..., :W].reshape(B, T, H, Dh)) * (Dh ** -0.5)
    k = _l2norm(qkv[..., W:2 * W].reshape(B, T, H, Dh))
    v = qkv[..., 2 * W:].reshape(B, T, H, Dh)
    beta = jax.nn.sigmoid(b_raw)
    g = -jnp.exp(a_log.astype(jnp.float32)) * jax.nn.softplus(a_raw + dt_bias.astype(jnp.float32))
    o = _chunk_gated_delta(q, k, v, g, beta)
    o = _rms(o, o_gain) * jax.nn.silu(z)
    return o.reshape(B, T, W).astype(h.dtype) @ w_out


def _sb_mixer(h, k_sh, v_sh, w_in, q_gain, w_out):
    B, T, _ = h.shape
    W, H, Dh = SB_WIDTH, SB_HEADS, SB_HEAD_DIM
    proj = h @ w_in
    q = _rms(proj[..., :W].reshape(B, T, H, Dh), q_gain).transpose(0, 2, 1, 3)
    z = proj[..., W:].astype(jnp.float32)
    outs = []
    for blk in range(T // Q_BLOCK):
        s0 = blk * Q_BLOCK
        end = s0 + Q_BLOCK
        kb, vb = k_sh[:, :, :end], v_sh[:, :, :end]
        logits = jnp.einsum("bhtd,bhsd->bhts", q[:, :, s0:end], kb) * (Dh ** -0.5)
        t_idx = s0 + jnp.arange(Q_BLOCK)
        s_idx = jnp.arange(end)
        mask = s_idx[None, :] < t_idx[:, None]
        log_keep = jnp.where(mask, jax.nn.log_sigmoid(-logits), 0.0)
        after = jnp.sum(log_keep, axis=-1, keepdims=True) - jnp.cumsum(log_keep, axis=-1)
        wts = jnp.where(mask, jnp.exp(jax.nn.log_sigmoid(logits) + after), 0.0)
        outs.append(jnp.einsum("bhts,bhsd->bhtd", wts, vb))
    o = jnp.concatenate(outs, axis=2).transpose(0, 2, 1, 3).reshape(B, T, W)
    o = o * jax.nn.silu(z)
    return o.astype(h.dtype) @ w_out


def setup_inputs(seed: int = 0) -> dict:
    key = jax.random.key(seed)
    ks = jax.random.split(key, 24)
    D = D_MODEL
    nrm = lambda k, shape, s: jax.random.normal(k, shape, jnp.float32) * s
    dt = jnp.exp(jax.random.uniform(ks[8], (N_A, GDN_HEADS), minval=np.log(1e-3), maxval=np.log(1e-1)))
    return {
        "x": nrm(ks[0], (BATCH, SEQ, D), 1.0),
        "c": nrm(ks[1], (BATCH, D), 1.0),
        "norm_g": 1.0 + nrm(ks[2], (DEPTH, D), 0.02),
        "w_ada": nrm(ks[3], (DEPTH, D, 3 * D), 0.5 * D ** -0.5),
        "b_ada": nrm(ks[4], (DEPTH, 3 * D), 0.02),
        "w_in_a": nrm(ks[5], (N_A, D, 4 * GDN_WIDTH + 2 * GDN_HEADS), D ** -0.5),
        "conv_w_a": nrm(ks[6], (N_A, CONV_WIDTH, 3 * GDN_WIDTH), CONV_WIDTH ** -0.5),
        "a_log_a": jnp.log(jax.random.uniform(ks[7], (N_A, GDN_HEADS), minval=1.0, maxval=16.0)),
        "dt_bias_a": dt + jnp.log(-jnp.expm1(-dt)),
        "o_gain_a": 1.0 + nrm(ks[9], (N_A, GDN_HEAD_DIM), 0.02),
        "w_out_a": nrm(ks[10], (N_A, GDN_WIDTH, D), GDN_WIDTH ** -0.5),
        "kv_norm_g": 1.0 + nrm(ks[11], (D,), 0.02),
        "w_ada_kv": nrm(ks[12], (D, 2 * D), 0.5 * D ** -0.5),
        "b_ada_kv": nrm(ks[13], (2 * D,), 0.02),
        "w_kv": nrm(ks[14], (D, 2 * SB_WIDTH), D ** -0.5),
        "k_gain": 1.0 + nrm(ks[15], (SB_HEAD_DIM,), 0.02),
        "w_in_b": nrm(ks[16], (N_B, D, 2 * SB_WIDTH), D ** -0.5),
        "q_gain_b": 1.0 + nrm(ks[17], (N_B, SB_HEAD_DIM), 0.02),
        "w_out_b": nrm(ks[18], (N_B, SB_WIDTH, D), SB_WIDTH ** -0.5),
    }


def reference(x, c, norm_g, w_ada, b_ada, w_in_a, conv_w_a, a_log_a, dt_bias_a, o_gain_a,
              w_out_a, kv_norm_g, w_ada_kv, b_ada_kv, w_kv, k_gain, w_in_b, q_gain_b, w_out_b):
    B, T, D = x.shape
    c_act = jax.nn.silu(c)
    k_sh = None
    v_sh = None
    for layer in range(DEPTH):
        mod = c_act @ w_ada[layer] + b_ada[layer]
        shift, scale, gate = mod[:, :D], mod[:, D:2 * D], mod[:, 2 * D:]
        h = _modulated_norm(x, norm_g[layer], shift, scale)
        if layer < N_A:
            i = layer
            y = _gdn_mixer(h, w_in_a[i], conv_w_a[i], a_log_a[i], dt_bias_a[i], o_gain_a[i], w_out_a[i])
        else:
            if layer == N_A:
                mkv = c_act @ w_ada_kv + b_ada_kv
                hk = _modulated_norm(x, kv_norm_g, mkv[:, :D], mkv[:, D:])
                kv = hk @ w_kv
                k_sh = _rms(kv[..., :SB_WIDTH].reshape(B, T, SB_HEADS, SB_HEAD_DIM), k_gain).transpose(0, 2, 1, 3)
                v_sh = kv[..., SB_WIDTH:].astype(jnp.float32).reshape(B, T, SB_HEADS, SB_HEAD_DIM).transpose(0, 2, 1, 3)
            i = layer - N_A
            y = _sb_mixer(h, k_sh, v_sh, w_in_b[i], q_gain_b[i], w_out_b[i])
        x = x + (gate[:, None, :] * y).astype(x.dtype)
    return x
```

```python
import functools

import jax
import jax.numpy as jnp
from jax import lax
from jax.experimental import pallas as pl
from jax.experimental.pallas import tpu as pltpu

F32 = jnp.float32
BF16 = jnp.bfloat16
HI = lax.Precision.HIGHEST

EPS = 1e-6
CHUNK = 64
GDN_HEADS = 8
GDN_HEAD_DIM = 128
CONV_WIDTH = 4
SB_HEADS = 16
SB_HEAD_DIM = 64
Q_BLOCK = 128
LANES = 128
VMEM_LIMIT = 48 * 1024 * 1024
ROW_TILE = 256
N_CHUNK = 512


def _params(n_axes):
    return pltpu.CompilerParams(dimension_semantics=("arbitrary",) * n_axes,
                                vmem_limit_bytes=VMEM_LIMIT)


def _silu(x):
    return x * jax.nn.sigmoid(x)


def _dot(a, b, precision=None):
    return jnp.dot(a, b, precision=precision, preferred_element_type=F32)


def _dot_nt(a, b, precision=None):
    return lax.dot_general(a, b, (((1,), (1,)), ((), ())), precision=precision,
                           preferred_element_type=F32)


def _mod_kernel(c_ref, w_ref, b_ref, o_ref):
    ca = _silu(c_ref[...])
    o_ref[...] = _dot(ca, w_ref[...], HI) + b_ref[...]


def _mod_call(c, w, b):
    L, D, N = w.shape
    B = c.shape[0]
    tn = 1024
    return pl.pallas_call(
        _mod_kernel,
        grid=(L, N // tn),
        in_specs=[pl.BlockSpec((B, D), lambda l, j: (0, 0)),
                  pl.BlockSpec((None, D, tn), lambda l, j: (l, 0, j)),
                  pl.BlockSpec((None, 1, tn), lambda l, j: (l, 0, j))],
        out_specs=pl.BlockSpec((None, B, tn), lambda l, j: (l, 0, j)),
        out_shape=jax.ShapeDtypeStruct((L, B, N), F32),
        compiler_params=_params(2),
        name="adaln_mod",
    )(c, w, b.reshape(L, 1, N))


def _modnorm(x_ref, g_ref, sh_ref, sc_ref):
    x = x_ref[...]
    n = x * lax.rsqrt(jnp.mean(x * x, axis=-1, keepdims=True) + EPS) * g_ref[...]
    return (n * (1.0 + sc_ref[...]) + sh_ref[...]).astype(BF16)


def _proj_cols(hb, w_ref, o_ref):
    n = w_ref.shape[1]
    for j in range(0, n, N_CHUNK):
        o_ref[:, j:j + N_CHUNK] = _dot(hb, w_ref[:, j:j + N_CHUNK])


def _norm_proj_kernel(x_ref, g_ref, sh_ref, sc_ref, w_ref, o_ref):
    _proj_cols(_modnorm(x_ref, g_ref, sh_ref, sc_ref), w_ref, o_ref)


def _norm_proj_tail_kernel(x_ref, g_ref, sh_ref, sc_ref, w_ref, wt_ref, o_ref, ot_ref):
    hb = _modnorm(x_ref, g_ref, sh_ref, sc_ref)
    _proj_cols(hb, w_ref, o_ref)
    ot_ref[...] = _dot(hb, wt_ref[...])


def _norm_kv_kernel(x_ref, g_ref, sh_ref, sc_ref, w_ref, kg_ref, o_ref):
    hb = _modnorm(x_ref, g_ref, sh_ref, sc_ref)
    n = w_ref.shape[1]
    half = n // 2
    tm = x_ref.shape[0]
    low = lax.broadcasted_iota(jnp.int32, (tm, LANES), 1) < SB_HEAD_DIM
    for j in range(0, half, LANES):
        kk = _dot(hb, w_ref[:, j:j + LANES])
        sq = kk * kk
        s_lo = jnp.sum(jnp.where(low, sq, 0.0), axis=-1, keepdims=True)
        s_hi = jnp.sum(jnp.where(low, 0.0, sq), axis=-1, keepdims=True)
        ms = jnp.where(low, s_lo, s_hi) * (1.0 / SB_HEAD_DIM)
        o_ref[:, j:j + LANES] = kk * lax.rsqrt(ms + EPS) * kg_ref[...]
    for j in range(half, n, N_CHUNK):
        o_ref[:, j:j + N_CHUNK] = _dot(hb, w_ref[:, j:j + N_CHUNK])


def _norm_proj_call(kernel_fn, x, gain, mod4, layer, w_list, extra=(), name="norm_proj"):
    B, T, D = x.shape
    tm = ROW_TILE
    in_specs = [pl.BlockSpec((None, tm, D), lambda b, i: (b, i, 0)),
                pl.BlockSpec((1, D), lambda b, i: (0, 0)),
                pl.BlockSpec((None, None, 1, D), lambda b, i: (layer, b, 0, 0)),
                pl.BlockSpec((None, None, 1, D), lambda b, i: (layer, b, 0, 1))]
    out_specs, out_shape = [], []
    for w in w_list:
        n = w.shape[1]
        in_specs.append(pl.BlockSpec((D, n), lambda b, i: (0, 0)))
        out_specs.append(pl.BlockSpec((None, tm, n), lambda b, i: (b, i, 0)))
        out_shape.append(jax.ShapeDtypeStruct((B, T, n), F32))
    for e in extra:
        in_specs.append(pl.BlockSpec(e.shape, lambda b, i: (0, 0)))
    return pl.pallas_call(
        kernel_fn,
        grid=(B, T // tm),
        in_specs=in_specs,
        out_specs=out_specs,
        out_shape=out_shape,
        compiler_params=_params(2),
        name=name,
    )(x, gain.reshape(1, D), mod4, mod4, *w_list, *extra)


def _outproj_kernel(o_ref, w_ref, x_ref, gate_ref, xo_ref):
    y = _dot(o_ref[...].astype(BF16), w_ref[...])
    xo_ref[...] = x_ref[...] + gate_ref[...] * y


def _outproj_call(o, w, x, mod4, layer):
    B, T, D = x.shape
    W = o.shape[-1]
    tm = ROW_TILE
    return pl.pallas_call(
        _outproj_kernel,
        grid=(B, T // tm),
        in_specs=[pl.BlockSpec((None, tm, W), lambda b, i: (b, i, 0)),
                  pl.BlockSpec((W, D), lambda b, i: (0, 0)),
                  pl.BlockSpec((None, tm, D), lambda b, i: (b, i, 0)),
                  pl.BlockSpec((None, None, 1, D), lambda b, i: (layer, b, 0, 2))],
        out_specs=pl.BlockSpec((None, tm, D), lambda b, i: (b, i, 0)),
        out_shape=jax.ShapeDtypeStruct((B, T, D), F32),
        compiler_params=_params(2),
        name="outproj_residual",
    )(o, w, x, mod4)


def _unit_lower_inverse(L):
    n = L.shape[0]
    ri = lax.broadcasted_iota(jnp.int32, (n, n), 0)
    ci = lax.broadcasted_iota(jnp.int32, (n, n), 1)
    X = jnp.where(ri == ci, 1.0, 0.0) - L
    P = L
    k = 2
    while k < n:
        P = _dot(P, P, HI)
        X = X + _dot(X, P, HI)
        k *= 2
    return X


def _gdn_kernel(q_ref, k_ref, v_ref, z_ref, ba_ref, cwq_ref, cwk_ref, cwv_ref,
                alog_ref, dtb_ref, og_ref, o_ref, qs, ks, vs, gs, bs):
    h = pl.program_id(1)
    T, Dh = q_ref.shape
    C = CHUNK
    row = lax.broadcasted_iota(jnp.int32, (T, Dh), 0)
    lane = lax.broadcasted_iota(jnp.int32, (T, LANES), 1)

    def conv_silu(x_ref, cw_ref):
        x = x_ref[...]
        cw = cw_ref[...]
        acc = x * cw[CONV_WIDTH - 1:CONV_WIDTH, :]
        for s in range(1, CONV_WIDTH):
            xs = jnp.where(row >= s, pltpu.roll(x, s, axis=0), 0.0)
            acc = acc + xs * cw[CONV_WIDTH - 1 - s:CONV_WIDTH - s, :]
        return _silu(acc)

    q = conv_silu(q_ref, cwq_ref)
    qs[...] = q * lax.rsqrt(jnp.sum(q * q, axis=-1, keepdims=True) + EPS) * (Dh ** -0.5)
    k = conv_silu(k_ref, cwk_ref)
    ks[...] = k * lax.rsqrt(jnp.sum(k * k, axis=-1, keepdims=True) + EPS)
    vs[...] = conv_silu(v_ref, cwv_ref)

    ba = ba_ref[...]
    beta_all = jax.nn.sigmoid(ba)
    a = ba + dtb_ref[...]
    softplus = jnp.maximum(a, 0.0) + jnp.log1p(jnp.exp(-jnp.abs(a)))
    g_all = -jnp.exp(alog_ref[...]) * softplus
    bs[...] = jnp.sum(jnp.where(lane == h, beta_all, 0.0), axis=-1, keepdims=True)
    gs[...] = jnp.sum(jnp.where(lane == h + GDN_HEADS, g_all, 0.0), axis=-1, keepdims=True)

    ri = lax.broadcasted_iota(jnp.int32, (C, C), 0)
    ci = lax.broadcasted_iota(jnp.int32, (C, C), 1)
    tril = ri >= ci
    strict = ri > ci
    eye = ri == ci

    def to_row(col):
        return jnp.sum(jnp.where(eye, jnp.broadcast_to(col, (C, C)), 0.0), axis=0, keepdims=True)

    def chunk(c, S):
        r0 = pl.multiple_of(c * C, C)
        q = qs[pl.ds(r0, C), :]
        k = ks[pl.ds(r0, C), :]
        v = vs[pl.ds(r0, C), :]
        beta = bs[pl.ds(r0, C), :]
        g_row = to_row(gs[pl.ds(r0, C), :])
        gc_col = jnp.sum(jnp.where(tril, jnp.broadcast_to(g_row, (C, C)), 0.0),
                         axis=1, keepdims=True)
        gc_row = to_row(gc_col)
        g_last = gc_col[C - 1:C, :]
        decay = jnp.where(tril, jnp.exp(jnp.where(tril, gc_col - gc_row, 0.0)), 0.0)
        kb = k * beta
        vb = v * beta
        L = jnp.where(strict, _dot_nt(kb, k, HI) * decay, 0.0)
        tinv = _unit_lower_inverse(L)
        egc = jnp.exp(gc_col)
        u = _dot(tinv, vb, HI)
        w = _dot(tinv, kb * egc, HI)
        attn = jnp.where(tril, _dot_nt(q, k, HI) * decay, 0.0)
        q_dec = q * egc
        k_dec = k * jnp.exp(g_last - gc_col)
        v_new = u - _dot(w, S, HI)
        o = _dot(q_dec, S, HI) + _dot(attn, v_new, HI)
        S = S * jnp.exp(g_last) + _dot(k_dec.T, v_new, HI)
        z = z_ref[pl.ds(r0, C), :]
        on = o * lax.rsqrt(jnp.mean(o * o, axis=-1, keepdims=True) + EPS) * og_ref[...]
        o_ref[pl.ds(r0, C), :] = on * _silu(z)
        return S

    lax.fori_loop(0, T // C, chunk, jnp.zeros((Dh, Dh), F32))


def _gdn_call(proj, ba, conv_w, a_log, dt_bias, o_gain):
    B, T, _ = proj.shape
    H, Dh = GDN_HEADS, GDN_HEAD_DIM
    pad = lambda vec: jnp.zeros((1, LANES), F32).at[0, H:2 * H].set(vec)
    col = lambda off: pl.BlockSpec((None, T, Dh), lambda b, h: (b, 0, off + h))
    cw = lambda off: pl.BlockSpec((CONV_WIDTH, Dh), lambda b, h: (0, off + h))
    row = pl.BlockSpec((1, LANES), lambda b, h: (0, 0))
    return pl.pallas_call(
        _gdn_kernel,
        grid=(B, H),
        in_specs=[col(0), col(H), col(2 * H), col(3 * H),
                  pl.BlockSpec((None, T, LANES), lambda b, h: (b, 0, 0)),
                  cw(0), cw(H), cw(2 * H), row, row, row],
        out_specs=pl.BlockSpec((None, T, Dh), lambda b, h: (b, 0, h)),
        out_shape=jax.ShapeDtypeStruct((B, T, H * Dh), F32),
        scratch_shapes=[pltpu.VMEM((T, Dh), F32)] * 3 + [pltpu.VMEM((T, 1), F32)] * 2,
        compiler_params=_params(2),
        name="gated_delta",
    )(proj, proj, proj, proj, ba, conv_w, conv_w, conv_w,
      pad(a_log), pad(dt_bias), o_gain.reshape(1, Dh))


def _split3(x):
    hi = x.astype(BF16)
    r = x - hi.astype(F32)
    mid = r.astype(BF16)
    lo = (r - mid.astype(F32)).astype(BF16)
    return hi, mid, lo


def _sb_kernel(q_ref, k_ref, v_ref, z_ref, qg_ref, o_ref):
    qi = pl.program_id(2)
    QB, Dh = Q_BLOCK, SB_HEAD_DIM
    ri = lax.broadcasted_iota(jnp.int32, (QB, QB), 0)
    ci = lax.broadcasted_iota(jnp.int32, (QB, QB), 1)
    later = jnp.where(ri > ci, 1.0, 0.0).astype(BF16)
    q2 = q_ref[...]
    outs = []
    for hh in range(LANES // Dh):
        sl = slice(hh * Dh, (hh + 1) * Dh)
        qh = q2[:, sl]
        qh = qh * lax.rsqrt(jnp.mean(qh * qh, axis=-1, keepdims=True) + EPS) * qg_ref[...]
        qb = (qh * (Dh ** -0.5)).astype(BF16)

        def body(jj, carry, sl=sl, qb=qb):
            acc, run = carry
            j = qi - jj
            r0 = pl.multiple_of(j * QB, QB)
            kb = k_ref[pl.ds(r0, QB), sl].astype(BF16)
            vb = v_ref[pl.ds(r0, QB), sl].astype(BF16)
            l = _dot_nt(qb, kb)
            mask = (j * QB + ci) < (qi * QB + ri)
            t = jnp.log1p(jnp.exp(-jnp.abs(l)))
            log_keep = jnp.where(mask, -jnp.maximum(l, 0.0) - t, 0.0)
            hi, mid, lo = _split3(log_keep)
            inner = _dot(hi, later) + _dot(mid, later) + _dot(lo, later)
            log_w = jnp.minimum(l, 0.0) - t + (run + inner)
            w = jnp.where(mask, jnp.exp(log_w), 0.0)
            acc = acc + _dot(w.astype(BF16), vb)
            run = run + jnp.sum(log_keep, axis=-1, keepdims=True)
            return acc, run

        acc, _ = lax.fori_loop(0, qi + 1, body,
                               (jnp.zeros((QB, Dh), F32), jnp.zeros((QB, 1), F32)))
        outs.append(acc)
    o_ref[...] = jnp.concatenate(outs, axis=1) * _silu(z_ref[...])


def _sb_call(proj, kv, q_gain):
    B, T, W2 = proj.shape
    W = W2 // 2
    nb = W // LANES
    return pl.pallas_call(
        _sb_kernel,
        grid=(B, nb, T // Q_BLOCK),
        in_specs=[pl.BlockSpec((None, Q_BLOCK, LANES), lambda b, p, i: (b, i, p)),
                  pl.BlockSpec((None, T, LANES), lambda b, p, i: (b, 0, p)),
                  pl.BlockSpec((None, T, LANES), lambda b, p, i: (b, 0, nb + p)),
                  pl.BlockSpec((None, Q_BLOCK, LANES), lambda b, p, i: (b, i, nb + p)),
                  pl.BlockSpec((1, SB_HEAD_DIM), lambda b, p, i: (0, 0))],
        out_specs=pl.BlockSpec((None, Q_BLOCK, LANES), lambda b, p, i: (b, i, p)),
        out_shape=jax.ShapeDtypeStruct((B, T, W), F32),
        compiler_params=_params(3),
        name="stick_breaking",
    )(proj, kv, kv, proj, q_gain.reshape(1, SB_HEAD_DIM))


def kernel(x, c, norm_g, w_ada, b_ada, w_in_a, conv_w_a, a_log_a, dt_bias_a, o_gain_a, w_out_a,
           kv_norm_g, w_ada_kv, b_ada_kv, w_kv, k_gain, w_in_b, q_gain_b, w_out_b):
    B, T, D = x.shape
    depth = w_ada.shape[0]
    n_a = w_in_a.shape[0]
    gw = GDN_HEADS * GDN_HEAD_DIM

    mod4 = _mod_call(c, w_ada, b_ada).reshape(depth, B, 1, 3 * D)
    modkv4 = _mod_call(c, w_ada_kv[None], b_ada_kv[None]).reshape(1, B, 1, 2 * D)

    kv = None
    for layer in range(depth):
        if layer < n_a:
            i = layer
            w_main = w_in_a[i, :, :4 * gw].astype(BF16)
            w_tail = jnp.pad(w_in_a[i, :, 4 * gw:], ((0, 0), (0, LANES - 2 * GDN_HEADS))).astype(BF16)
            proj, ba = _norm_proj_call(_norm_proj_tail_kernel, x, norm_g[layer], mod4, layer,
                                       [w_main, w_tail], name="gdn_in_proj")
            o = _gdn_call(proj, ba, conv_w_a[i], a_log_a[i], dt_bias_a[i], o_gain_a[i])
            x = _outproj_call(o, w_out_a[i].astype(BF16), x, mod4, layer)
        else:
            if kv is None:
                kg = jnp.tile(k_gain, LANES // SB_HEAD_DIM).reshape(1, LANES)
                (kv,) = _norm_proj_call(_norm_kv_kernel, x, kv_norm_g, modkv4, 0,
                                        [w_kv.astype(BF16)], extra=(kg,), name="shared_kv")
            i = layer - n_a
            (proj,) = _norm_proj_call(_norm_proj_kernel, x, norm_g[layer], mod4, layer,
                                      [w_in_b[i].astype(BF16)], name="sb_in_proj")
            o = _sb_call(proj, kv, q_gain_b[i])
            x = _outproj_call(o, w_out_b[i].astype(BF16), x, mod4, layer)
    return x
```

```python
import functools

import jax
import jax.numpy as jnp
from jax import lax
from jax.experimental import pallas as pl
from jax.experimental.pallas import tpu as pltpu

F32 = jnp.float32
BF16 = jnp.bfloat16
HI = lax.Precision.HIGHEST

EPS = 1e-6
CHUNK = 64
GDN_HEADS = 8
GDN_HEAD_DIM = 128
CONV_WIDTH = 4
SB_HEADS = 16
SB_HEAD_DIM = 64
Q_BLOCK = 128
LANES = 128
VMEM_LIMIT = 48 * 1024 * 1024
ROW_TILE = 256
N_CHUNK = 512


def _params(n_axes):
    return pltpu.CompilerParams(dimension_semantics=("arbitrary",) * n_axes,
                                vmem_limit_bytes=VMEM_LIMIT)


def _silu(x):
    return x * jax.nn.sigmoid(x)


def _dot(a, b, precision=None):
    return jnp.dot(a, b, precision=precision, preferred_element_type=F32)


def _dot_nt(a, b, precision=None):
    return lax.dot_general(a, b, (((1,), (1,)), ((), ())), precision=precision,
                           preferred_element_type=F32)


def _mod_kernel(c_ref, w_ref, b_ref, o_ref):
    ca = _silu(c_ref[...])
    o_ref[...] = _dot(ca, w_ref[...], HI) + b_ref[...]


def _mod_call(c, w, b):
    L, D, N = w.shape
    B = c.shape[0]
    tn = 1024
    return pl.pallas_call(
        _mod_kernel,
        grid=(L, N // tn),
        in_specs=[pl.BlockSpec((B, D), lambda l, j: (0, 0)),
                  pl.BlockSpec((None, D, tn), lambda l, j: (l, 0, j)),
                  pl.BlockSpec((None, 1, tn), lambda l, j: (l, 0, j))],
        out_specs=pl.BlockSpec((None, B, tn), lambda l, j: (l, 0, j)),
        out_shape=jax.ShapeDtypeStruct((L, B, N), F32),
        compiler_params=_params(2),
        name="adaln_mod",
    )(c, w, b.reshape(L, 1, N))


def _modnorm(x_ref, g_ref, sh_ref, sc_ref):
    x = x_ref[...]
    n = x * lax.rsqrt(jnp.mean(x * x, axis=-1, keepdims=True) + EPS) * g_ref[...]
    return (n * (1.0 + sc_ref[...]) + sh_ref[...]).astype(BF16)


def _proj_cols(hb, w_ref, o_ref):
    n = w_ref.shape[1]
    for j in range(0, n, N_CHUNK):
        o_ref[:, j:j + N_CHUNK] = _dot(hb, w_ref[:, j:j + N_CHUNK])


def _norm_proj_kernel(x_ref, g_ref, sh_ref, sc_ref, w_ref, o_ref):
    _proj_cols(_modnorm(x_ref, g_ref, sh_ref, sc_ref), w_ref, o_ref)


def _norm_proj_tail_kernel(x_ref, g_ref, sh_ref, sc_ref, w_ref, wt_ref, o_ref, ot_ref):
    hb = _modnorm(x_ref, g_ref, sh_ref, sc_ref)
    _proj_cols(hb, w_ref, o_ref)
    ot_ref[...] = _dot(hb, wt_ref[...])


def _norm_kv_kernel(x_ref, g_ref, sh_ref, sc_ref, w_ref, kg_ref, o_ref):
    hb = _modnorm(x_ref, g_ref, sh_ref, sc_ref)
    n = w_ref.shape[1]
    half = n // 2
    tm = x_ref.shape[0]
    low = lax.broadcasted_iota(jnp.int32, (tm, LANES), 1) < SB_HEAD_DIM
    for j in range(0, half, LANES):
        kk = _dot(hb, w_ref[:, j:j + LANES])
        sq = kk * kk
        s_lo = jnp.sum(jnp.where(low, sq, 0.0), axis=-1, keepdims=True)
        s_hi = jnp.sum(jnp.where(low, 0.0, sq), axis=-1, keepdims=True)
        ms = jnp.where(low, s_lo, s_hi) * (1.0 / SB_HEAD_DIM)
        o_ref[:, j:j + LANES] = (kk * lax.rsqrt(ms + EPS) * kg_ref[...]).astype(o_ref.dtype)
    for j in range(half, n, N_CHUNK):
        o_ref[:, j:j + N_CHUNK] = _dot(hb, w_ref[:, j:j + N_CHUNK]).astype(o_ref.dtype)


def _norm_proj_call(kernel_fn, x, gain, mod4, layer, w_list, extra=(), name="norm_proj",
                    out_dtype=F32):
    B, T, D = x.shape
    tm = ROW_TILE
    in_specs = [pl.BlockSpec((None, tm, D), lambda b, i: (b, i, 0)),
                pl.BlockSpec((1, D), lambda b, i: (0, 0)),
                pl.BlockSpec((None, None, 1, D), lambda b, i: (layer, b, 0, 0)),
                pl.BlockSpec((None, None, 1, D), lambda b, i: (layer, b, 0, 1))]
    out_specs, out_shape = [], []
    for w in w_list:
        n = w.shape[1]
        in_specs.append(pl.BlockSpec((D, n), lambda b, i: (0, 0)))
        out_specs.append(pl.BlockSpec((None, tm, n), lambda b, i: (b, i, 0)))
        out_shape.append(jax.ShapeDtypeStruct((B, T, n), out_dtype))
    for e in extra:
        in_specs.append(pl.BlockSpec(e.shape, lambda b, i: (0, 0)))
    return pl.pallas_call(
        kernel_fn,
        grid=(B, T // tm),
        in_specs=in_specs,
        out_specs=out_specs,
        out_shape=out_shape,
        compiler_params=_params(2),
        name=name,
    )(x, gain.reshape(1, D), mod4, mod4, *w_list, *extra)


def _outproj_kernel(o_ref, w_ref, x_ref, gate_ref, xo_ref):
    y = _dot(o_ref[...].astype(BF16), w_ref[...])
    xo_ref[...] = x_ref[...] + gate_ref[...] * y


def _outproj_call(o, w, x, mod4, layer):
    B, T, D = x.shape
    W = o.shape[-1]
    tm = ROW_TILE
    return pl.pallas_call(
        _outproj_kernel,
        grid=(B, T // tm),
        in_specs=[pl.BlockSpec((None, tm, W), lambda b, i: (b, i, 0)),
                  pl.BlockSpec((W, D), lambda b, i: (0, 0)),
                  pl.BlockSpec((None, tm, D), lambda b, i: (b, i, 0)),
                  pl.BlockSpec((None, None, 1, D), lambda b, i: (layer, b, 0, 2))],
        out_specs=pl.BlockSpec((None, tm, D), lambda b, i: (b, i, 0)),
        out_shape=jax.ShapeDtypeStruct((B, T, D), F32),
        compiler_params=_params(2),
        name="outproj_residual",
    )(o, w, x, mod4)


def _unit_lower_inverse(L):
    n = L.shape[0]
    ri = lax.broadcasted_iota(jnp.int32, (n, n), 0)
    ci = lax.broadcasted_iota(jnp.int32, (n, n), 1)
    X = jnp.where(ri == ci, 1.0, 0.0) - L
    P = L
    k = 2
    while k < n:
        P = _dot(P, P, HI)
        X = X + _dot(X, P, HI)
        k *= 2
    return X


def _gdn_kernel(q_ref, k_ref, v_ref, z_ref, ba_ref, cwq_ref, cwk_ref, cwv_ref,
                alog_ref, dtb_ref, og_ref, o_ref, qs, ks, vs, gs, bs):
    h = pl.program_id(1)
    T, Dh = q_ref.shape
    C = CHUNK
    row = lax.broadcasted_iota(jnp.int32, (T, Dh), 0)
    lane = lax.broadcasted_iota(jnp.int32, (T, LANES), 1)

    def conv_silu(x_ref, cw_ref):
        x = x_ref[...]
        cw = cw_ref[...]
        acc = x * cw[CONV_WIDTH - 1:CONV_WIDTH, :]
        for s in range(1, CONV_WIDTH):
            xs = jnp.where(row >= s, pltpu.roll(x, s, axis=0), 0.0)
            acc = acc + xs * cw[CONV_WIDTH - 1 - s:CONV_WIDTH - s, :]
        return _silu(acc)

    q = conv_silu(q_ref, cwq_ref)
    qs[...] = q * lax.rsqrt(jnp.sum(q * q, axis=-1, keepdims=True) + EPS) * (Dh ** -0.5)
    k = conv_silu(k_ref, cwk_ref)
    ks[...] = k * lax.rsqrt(jnp.sum(k * k, axis=-1, keepdims=True) + EPS)
    vs[...] = conv_silu(v_ref, cwv_ref)

    ba = ba_ref[...]
    beta_all = jax.nn.sigmoid(ba)
    a = ba + dtb_ref[...]
    softplus = jnp.maximum(a, 0.0) + jnp.log1p(jnp.exp(-jnp.abs(a)))
    g_all = -jnp.exp(alog_ref[...]) * softplus
    bs[...] = jnp.sum(jnp.where(lane == h, beta_all, 0.0), axis=-1, keepdims=True)
    gs[...] = jnp.sum(jnp.where(lane == h + GDN_HEADS, g_all, 0.0), axis=-1, keepdims=True)

    ri = lax.broadcasted_iota(jnp.int32, (C, C), 0)
    ci = lax.broadcasted_iota(jnp.int32, (C, C), 1)
    tril = ri >= ci
    strict = ri > ci
    eye = ri == ci

    def to_row(col):
        return jnp.sum(jnp.where(eye, jnp.broadcast_to(col, (C, C)), 0.0), axis=0, keepdims=True)

    def chunk(c, S):
        r0 = pl.multiple_of(c * C, C)
        q = qs[pl.ds(r0, C), :]
        k = ks[pl.ds(r0, C), :]
        v = vs[pl.ds(r0, C), :]
        beta = bs[pl.ds(r0, C), :]
        g_row = to_row(gs[pl.ds(r0, C), :])
        gc_col = jnp.sum(jnp.where(tril, jnp.broadcast_to(g_row, (C, C)), 0.0),
                         axis=1, keepdims=True)
        gc_row = to_row(gc_col)
        g_last = gc_col[C - 1:C, :]
        decay = jnp.where(tril, jnp.exp(jnp.where(tril, gc_col - gc_row, 0.0)), 0.0)
        kb = k * beta
        vb = v * beta
        L = jnp.where(strict, _dot_nt(kb, k, HI) * decay, 0.0)
        tinv = _unit_lower_inverse(L)
        egc = jnp.exp(gc_col)
        u = _dot(tinv, vb, HI)
        w = _dot(tinv, kb * egc, HI)
        attn = jnp.where(tril, _dot_nt(q, k, HI) * decay, 0.0)
        q_dec = q * egc
        k_dec = k * jnp.exp(g_last - gc_col)
        v_new = u - _dot(w, S, HI)
        o = _dot(q_dec, S, HI) + _dot(attn, v_new, HI)
        S = S * jnp.exp(g_last) + _dot(k_dec.T, v_new, HI)
        z = z_ref[pl.ds(r0, C), :]
        on = o * lax.rsqrt(jnp.mean(o * o, axis=-1, keepdims=True) + EPS) * og_ref[...]
        o_ref[pl.ds(r0, C), :] = on * _silu(z)
        return S

    lax.fori_loop(0, T // C, chunk, jnp.zeros((Dh, Dh), F32))


def _gdn_call(proj, ba, conv_w, a_log, dt_bias, o_gain):
    B, T, _ = proj.shape
    H, Dh = GDN_HEADS, GDN_HEAD_DIM
    pad = lambda vec: jnp.zeros((1, LANES), F32).at[0, H:2 * H].set(vec)
    col = lambda off: pl.BlockSpec((None, T, Dh), lambda b, h: (b, 0, off + h))
    cw = lambda off: pl.BlockSpec((CONV_WIDTH, Dh), lambda b, h: (0, off + h))
    row = pl.BlockSpec((1, LANES), lambda b, h: (0, 0))
    return pl.pallas_call(
        _gdn_kernel,
        grid=(B, H),
        in_specs=[col(0), col(H), col(2 * H), col(3 * H),
                  pl.BlockSpec((None, T, LANES), lambda b, h: (b, 0, 0)),
                  cw(0), cw(H), cw(2 * H), row, row, row],
        out_specs=pl.BlockSpec((None, T, Dh), lambda b, h: (b, 0, h)),
        out_shape=jax.ShapeDtypeStruct((B, T, H * Dh), F32),
        scratch_shapes=[pltpu.VMEM((T, Dh), F32)] * 3 + [pltpu.VMEM((T, 1), F32)] * 2,
        compiler_params=_params(2),
        name="gated_delta",
    )(proj, proj, proj, proj, ba, conv_w, conv_w, conv_w,
      pad(a_log), pad(dt_bias), o_gain.reshape(1, Dh))


SB_Q_PER_STEP = 2
SB_WINDOW = 2 * Q_BLOCK
LOG_TINY_F32 = -126.0 * 0.6931471805599453


def _sb_scores(qm2, kw, mask):
    l = _dot_nt(qm2, kw)
    t = jnp.log1p(jnp.exp(-jnp.abs(l)))
    log_keep = -jnp.maximum(l, 0.0) - t
    if mask is not None:
        log_keep = jnp.where(mask, log_keep, 0.0)
    return l, t, log_keep


def _split2(x):
    hi = x.astype(BF16)
    return hi, (x - hi.astype(F32)).astype(BF16)


def _sb_weights(l, t, after, mask):
    w = jnp.exp(jnp.minimum(l, 0.0) - t + after)
    if mask is not None:
        w = jnp.where(mask, w, 0.0)
    return w.astype(BF16)


def _sb_kernel(q_ref, k_ref, v_ref, z_ref, qg_ref, o_ref):
    step = pl.program_id(2)
    QB, Dh, WK = Q_BLOCK, SB_HEAD_DIM, SB_WINDOW
    R = 2 * QB
    ri = lax.broadcasted_iota(jnp.int32, (R, WK), 0) & (QB - 1)
    ci = lax.broadcasted_iota(jnp.int32, (R, WK), 1)
    rr = lax.broadcasted_iota(jnp.int32, (WK, WK), 0)
    cc = lax.broadcasted_iota(jnp.int32, (WK, WK), 1)
    later = jnp.where(rr > cc, 1.0, 0.0).astype(BF16)
    later_blk = later[:QB, :QB]
    low = lax.broadcasted_iota(jnp.int32, (QB, LANES), 1) < Dh

    blocks = []
    for qq in range(SB_Q_PER_STEP):
        qi = step * SB_Q_PER_STEP + qq
        q2 = q_ref[qq * QB:(qq + 1) * QB, :]
        sq = q2 * q2
        s_lo = jnp.sum(jnp.where(low, sq, 0.0), axis=-1, keepdims=True)
        s_hi = jnp.sum(jnp.where(low, 0.0, sq), axis=-1, keepdims=True)
        ms = jnp.where(low, s_lo, s_hi) * (1.0 / Dh)
        qn = q2 * lax.rsqrt(ms + EPS) * qg_ref[...] * (Dh ** -0.5)
        qm2 = jnp.concatenate([jnp.where(low, qn, 0.0), jnp.where(low, 0.0, qn)], axis=0).astype(BF16)
        w0 = jnp.maximum(qi - 1, 0)
        k0 = pl.multiple_of(w0 * QB, QB)
        mask = (w0 * QB + ci) < (qi * QB + ri)
        l, t, log_keep = _sb_scores(qm2, k_ref[pl.ds(k0, WK), :], mask)
        blocks.append((qm2, w0, k0, mask, l, t, log_keep))

    parts = [p for blk in blocks for p in _split2(blk[6])]
    sums = _dot(jnp.concatenate(parts, axis=0), later)

    def cond(c):
        j, _, run = c
        return jnp.logical_and(j >= 0, jnp.max(run) >= LOG_TINY_F32)

    heads = []
    for qq, (qm2, w0, k0, mask, l, t, log_keep) in enumerate(blocks):
        after = sums[2 * qq * R:(2 * qq + 1) * R] + sums[(2 * qq + 1) * R:(2 * qq + 2) * R]
        acc = _dot(_sb_weights(l, t, after, mask), v_ref[pl.ds(k0, WK), :])
        heads.append((acc, jnp.sum(log_keep, axis=-1, keepdims=True)))

    for qq, (qm2, w0, k0, mask, l, t, log_keep) in enumerate(blocks):
        acc, run = heads[qq]

        def body(c, qm2=qm2):
            j, acc, run = c
            r0 = pl.multiple_of(j * QB, QB)
            l, t, log_keep = _sb_scores(qm2, k_ref[pl.ds(r0, QB), :], None)
            hi, mid = _split2(log_keep)
            after = _dot(hi, later_blk) + _dot(mid, later_blk) + run
            acc = acc + _dot(_sb_weights(l, t, after, None), v_ref[pl.ds(r0, QB), :])
            return j - 1, acc, run + jnp.sum(log_keep, axis=-1, keepdims=True)

        acc = lax.while_loop(cond, body, (w0 - 1, acc, run))[1]
        rows = slice(qq * QB, (qq + 1) * QB)
        o_ref[rows, :] = jnp.where(low, acc[:QB], acc[QB:]) * _silu(z_ref[rows, :])


def _sb_call(proj, kv, q_gain):
    B, T, W2 = proj.shape
    W = W2 // 2
    nb = W // LANES
    tq = SB_Q_PER_STEP * Q_BLOCK
    qg = jnp.tile(q_gain, LANES // SB_HEAD_DIM).reshape(1, LANES)
    return pl.pallas_call(
        _sb_kernel,
        grid=(B, nb, T // tq),
        in_specs=[pl.BlockSpec((None, tq, LANES), lambda b, p, i: (b, i, p)),
                  pl.BlockSpec((None, T, LANES), lambda b, p, i: (b, 0, p)),
                  pl.BlockSpec((None, T, LANES), lambda b, p, i: (b, 0, nb + p)),
                  pl.BlockSpec((None, tq, LANES), lambda b, p, i: (b, i, nb + p)),
                  pl.BlockSpec((1, LANES), lambda b, p, i: (0, 0))],
        out_specs=pl.BlockSpec((None, tq, LANES), lambda b, p, i: (b, i, p)),
        out_shape=jax.ShapeDtypeStruct((B, T, W), F32),
        compiler_params=_params(3),
        name="stick_breaking",
    )(proj, kv, kv, proj, qg)


def kernel(x, c, norm_g, w_ada, b_ada, w_in_a, conv_w_a, a_log_a, dt_bias_a, o_gain_a, w_out_a,
           kv_norm_g, w_ada_kv, b_ada_kv, w_kv, k_gain, w_in_b, q_gain_b, w_out_b):
    B, T, D = x.shape
    depth = w_ada.shape[0]
    n_a = w_in_a.shape[0]
    gw = GDN_HEADS * GDN_HEAD_DIM

    mod4 = _mod_call(c, w_ada, b_ada).reshape(depth, B, 1, 3 * D)
    modkv4 = _mod_call(c, w_ada_kv[None], b_ada_kv[None]).reshape(1, B, 1, 2 * D)

    kv = None
    for layer in range(depth):
        if layer < n_a:
            i = layer
            w_main = w_in_a[i, :, :4 * gw].astype(BF16)
            w_tail = jnp.pad(w_in_a[i, :, 4 * gw:], ((0, 0), (0, LANES - 2 * GDN_HEADS))).astype(BF16)
            proj, ba = _norm_proj_call(_norm_proj_tail_kernel, x, norm_g[layer], mod4, layer,
                                       [w_main, w_tail], name="gdn_in_proj")
            o = _gdn_call(proj, ba, conv_w_a[i], a_log_a[i], dt_bias_a[i], o_gain_a[i])
            x = _outproj_call(o, w_out_a[i].astype(BF16), x, mod4, layer)
        else:
            if kv is None:
                kg = jnp.tile(k_gain, LANES // SB_HEAD_DIM).reshape(1, LANES)
                (kv,) = _norm_proj_call(_norm_kv_kernel, x, kv_norm_g, modkv4, 0,
                                        [w_kv.astype(BF16)], extra=(kg,), name="shared_kv",
                                        out_dtype=BF16)
            i = layer - n_a
            (proj,) = _norm_proj_call(_norm_proj_kernel, x, norm_g[layer], mod4, layer,
                                      [w_in_b[i].astype(BF16)], name="sb_in_proj")
            o = _sb_call(proj, kv, q_gain_b[i])
            x = _outproj_call(o, w_out_b[i].astype(BF16), x, mod4, layer)
    return x
```

```python
import functools

import jax
import jax.numpy as jnp
from jax import lax
from jax.experimental import pallas as pl
from jax.experimental.pallas import tpu as pltpu

F32 = jnp.float32
BF16 = jnp.bfloat16
HI = lax.Precision.HIGHEST

EPS = 1e-6
CHUNK = 64
GDN_HEADS = 8
GDN_HEAD_DIM = 128
CONV_WIDTH = 4
SB_HEADS = 16
SB_HEAD_DIM = 64
Q_BLOCK = 128
LANES = 128
VMEM_LIMIT = 48 * 1024 * 1024
ROW_TILE = 256
N_CHUNK = 512


def _params(n_axes):
    return pltpu.CompilerParams(dimension_semantics=("arbitrary",) * n_axes,
                                vmem_limit_bytes=VMEM_LIMIT)


def _silu(x):
    return x * jax.nn.sigmoid(x)


def _dot(a, b, precision=None):
    return jnp.dot(a, b, precision=precision, preferred_element_type=F32)


def _dot_nt(a, b, precision=None):
    return lax.dot_general(a, b, (((1,), (1,)), ((), ())), precision=precision,
                           preferred_element_type=F32)


def _mod_kernel(c_ref, w_ref, b_ref, o_ref):
    ca = _silu(c_ref[...])
    o_ref[...] = _dot(ca, w_ref[...], HI) + b_ref[...]


def _mod_call(c, w, b):
    L, D, N = w.shape
    B = c.shape[0]
    tn = 1024
    return pl.pallas_call(
        _mod_kernel,
        grid=(L, N // tn),
        in_specs=[pl.BlockSpec((B, D), lambda l, j: (0, 0)),
                  pl.BlockSpec((None, D, tn), lambda l, j: (l, 0, j)),
                  pl.BlockSpec((None, 1, tn), lambda l, j: (l, 0, j))],
        out_specs=pl.BlockSpec((None, B, tn), lambda l, j: (l, 0, j)),
        out_shape=jax.ShapeDtypeStruct((L, B, N), F32),
        compiler_params=_params(2),
        name="adaln_mod",
    )(c, w, b.reshape(L, 1, N))


def _modnorm(x_ref, g_ref, sh_ref, sc_ref):
    x = x_ref[...]
    n = x * lax.rsqrt(jnp.mean(x * x, axis=-1, keepdims=True) + EPS) * g_ref[...]
    return (n * (1.0 + sc_ref[...]) + sh_ref[...]).astype(BF16)


def _proj_cols(hb, w_ref, o_ref):
    n = w_ref.shape[1]
    for j in range(0, n, N_CHUNK):
        o_ref[:, j:j + N_CHUNK] = _dot(hb, w_ref[:, j:j + N_CHUNK])


def _norm_proj_kernel(x_ref, g_ref, sh_ref, sc_ref, w_ref, o_ref):
    _proj_cols(_modnorm(x_ref, g_ref, sh_ref, sc_ref), w_ref, o_ref)


def _norm_proj_tail_kernel(x_ref, g_ref, sh_ref, sc_ref, w_ref, wt_ref, o_ref, ot_ref):
    hb = _modnorm(x_ref, g_ref, sh_ref, sc_ref)
    _proj_cols(hb, w_ref, o_ref)
    ot_ref[...] = _dot(hb, wt_ref[...])


def _norm_kv_kernel(x_ref, g_ref, sh_ref, sc_ref, w_ref, kg_ref, o_ref):
    hb = _modnorm(x_ref, g_ref, sh_ref, sc_ref)
    n = w_ref.shape[1]
    half = n // 2
    tm = x_ref.shape[0]
    low = lax.broadcasted_iota(jnp.int32, (tm, LANES), 1) < SB_HEAD_DIM
    for j in range(0, half, LANES):
        kk = _dot(hb, w_ref[:, j:j + LANES])
        sq = kk * kk
        s_lo = jnp.sum(jnp.where(low, sq, 0.0), axis=-1, keepdims=True)
        s_hi = jnp.sum(jnp.where(low, 0.0, sq), axis=-1, keepdims=True)
        ms = jnp.where(low, s_lo, s_hi) * (1.0 / SB_HEAD_DIM)
        o_ref[:, j:j + LANES] = (kk * lax.rsqrt(ms + EPS) * kg_ref[...]).astype(o_ref.dtype)
    for j in range(half, n, N_CHUNK):
        o_ref[:, j:j + N_CHUNK] = _dot(hb, w_ref[:, j:j + N_CHUNK]).astype(o_ref.dtype)


def _norm_proj_call(kernel_fn, x, gain, mod4, layer, w_list, extra=(), name="norm_proj",
                    out_dtype=F32):
    B, T, D = x.shape
    tm = ROW_TILE
    in_specs = [pl.BlockSpec((None, tm, D), lambda b, i: (b, i, 0)),
                pl.BlockSpec((1, D), lambda b, i: (0, 0)),
                pl.BlockSpec((None, None, 1, D), lambda b, i: (layer, b, 0, 0)),
                pl.BlockSpec((None, None, 1, D), lambda b, i: (layer, b, 0, 1))]
    out_specs, out_shape = [], []
    for w in w_list:
        n = w.shape[1]
        in_specs.append(pl.BlockSpec((D, n), lambda b, i: (0, 0)))
        out_specs.append(pl.BlockSpec((None, tm, n), lambda b, i: (b, i, 0)))
        out_shape.append(jax.ShapeDtypeStruct((B, T, n), out_dtype))
    for e in extra:
        in_specs.append(pl.BlockSpec(e.shape, lambda b, i: (0, 0)))
    return pl.pallas_call(
        kernel_fn,
        grid=(B, T // tm),
        in_specs=in_specs,
        out_specs=out_specs,
        out_shape=out_shape,
        compiler_params=_params(2),
        name=name,
    )(x, gain.reshape(1, D), mod4, mod4, *w_list, *extra)


def _outproj_kernel(o_ref, w_ref, x_ref, gate_ref, xo_ref):
    y = _dot(o_ref[...].astype(BF16), w_ref[...])
    xo_ref[...] = x_ref[...] + gate_ref[...] * y


def _outproj_call(o, w, x, mod4, layer):
    B, T, D = x.shape
    W = o.shape[-1]
    tm = ROW_TILE
    return pl.pallas_call(
        _outproj_kernel,
        grid=(B, T // tm),
        in_specs=[pl.BlockSpec((None, tm, W), lambda b, i: (b, i, 0)),
                  pl.BlockSpec((W, D), lambda b, i: (0, 0)),
                  pl.BlockSpec((None, tm, D), lambda b, i: (b, i, 0)),
                  pl.BlockSpec((None, None, 1, D), lambda b, i: (layer, b, 0, 2))],
        out_specs=pl.BlockSpec((None, tm, D), lambda b, i: (b, i, 0)),
        out_shape=jax.ShapeDtypeStruct((B, T, D), F32),
        compiler_params=_params(2),
        name="outproj_residual",
    )(o, w, x, mod4)


GDN_HB = 4
GDN_TB = 512
GDN_PREP_CHUNKS = 2
HALO = 8


def _split2(x):
    hi = x.astype(BF16)
    return hi, (x - hi.astype(F32)).astype(BF16)


def _dot_split(lhs_rows, p):
    n = p.shape[0]
    m = len(lhs_rows)
    his, los = zip(*[_split2(a) for a in lhs_rows])
    ph, plo = _split2(p)
    r1 = _dot(jnp.concatenate(his + los, axis=0), ph)
    r2 = _dot(jnp.concatenate(his, axis=0), plo)
    return [r1[i * n:(i + 1) * n] + r1[(m + i) * n:(m + i + 1) * n] + r2[i * n:(i + 1) * n]
            for i in range(m)]


def _unit_lower_inverses(Ls):
    n = Ls[0].shape[0]
    ri = lax.broadcasted_iota(jnp.int32, (n, n), 0)
    ci = lax.broadcasted_iota(jnp.int32, (n, n), 1)
    eye = jnp.where(ri == ci, 1.0, 0.0)
    Xs = [eye - L for L in Ls]
    Ps = [_dot_split([L], L)[0] for L in Ls]
    k = 4
    while k < n:
        outs = [_dot_split([X, P], P) for X, P in zip(Xs, Ps)]
        Xs = [X + o[0] for X, o in zip(Xs, outs)]
        Ps = [o[1] for o in outs]
        k *= 2
    return [X + _dot_split([X], P)[0] for X, P in zip(Xs, Ps)]


def _gdn_kernel(q_ref, k_ref, v_ref, z_ref, ba_ref, cwq_ref, cwk_ref, cwv_ref,
                alog_ref, dtb_ref, og_ref, o_ref,
                state_s, xe_s, qn_s, kn_s, vn_s, beta_s, g_s, u_s, wq_s, ka_s, gl_s):
    hg = pl.program_id(1)
    tb = pl.program_id(2)
    TB, W = q_ref.shape
    HB, Dh, C = GDN_HB, GDN_HEAD_DIM, CHUNK
    NC = TB // C

    @pl.when(tb == 0)
    def _():
        state_s[...] = jnp.zeros_like(state_s)
        xe_s[:, :HALO, :] = jnp.zeros((3, HALO, W), F32)

    for idx, (x_ref, cw_ref, dst) in enumerate(((q_ref, cwq_ref, qn_s), (k_ref, cwk_ref, kn_s),
                                                (v_ref, cwv_ref, vn_s))):
        xe_s[idx, HALO:, :] = x_ref[...]
        cw = cw_ref[...]
        acc = x_ref[...] * cw[CONV_WIDTH - 1:CONV_WIDTH, :]
        for s in range(1, CONV_WIDTH):
            acc = acc + xe_s[idx, HALO - s:HALO - s + TB, :] * cw[CONV_WIDTH - 1 - s:CONV_WIDTH - s, :]
        xe_s[idx, :HALO, :] = xe_s[idx, TB:, :]
        y = _silu(acc)
        for h in range(HB):
            yh = y[:, h * Dh:(h + 1) * Dh]
            if idx == 0:
                yh = yh * lax.rsqrt(jnp.sum(yh * yh, axis=-1, keepdims=True) + EPS) * (Dh ** -0.5)
            elif idx == 1:
                yh = yh * lax.rsqrt(jnp.sum(yh * yh, axis=-1, keepdims=True) + EPS)
            dst[h] = yh

    lane = lax.broadcasted_iota(jnp.int32, (TB, LANES), 1)
    ba = ba_ref[...]
    beta_all = jax.nn.sigmoid(ba)
    a = ba + dtb_ref[...]
    g_all = -jnp.exp(alog_ref[...]) * (jnp.maximum(a, 0.0) + jnp.log1p(jnp.exp(-jnp.abs(a))))
    for h in range(HB):
        head = hg * HB + h
        beta_s[h] = jnp.sum(jnp.where(lane == head, beta_all, 0.0), axis=-1, keepdims=True)
        g_s[h] = jnp.sum(jnp.where(lane == head + GDN_HEADS, g_all, 0.0), axis=-1, keepdims=True)

    ri = lax.broadcasted_iota(jnp.int32, (C, C), 0)
    ci = lax.broadcasted_iota(jnp.int32, (C, C), 1)
    tril = ri >= ci
    strict = ri > ci
    eye = ri == ci

    def to_row(col):
        return jnp.sum(jnp.where(eye, jnp.broadcast_to(col, (C, C)), 0.0), axis=0, keepdims=True)

    def prepare(i):
        chains = [(h, i * GDN_PREP_CHUNKS + cc) for cc in range(GDN_PREP_CHUNKS) for h in range(HB)]
        staged = []
        for h, c in chains:
            r0 = c * C if isinstance(c, int) else pl.multiple_of(c * C, C)
            q = qn_s[h, pl.ds(r0, C), :]
            k = kn_s[h, pl.ds(r0, C), :]
            v = vn_s[h, pl.ds(r0, C), :]
            beta = beta_s[h, pl.ds(r0, C), :]
            g_row = to_row(g_s[h, pl.ds(r0, C), :])
            gc_col = jnp.sum(jnp.where(tril, jnp.broadcast_to(g_row, (C, C)), 0.0),
                             axis=1, keepdims=True)
            gc_row = to_row(gc_col)
            g_last = gc_col[C - 1:C, :]
            decay = jnp.where(tril, jnp.exp(jnp.where(tril, gc_col - gc_row, 0.0)), 0.0)
            kb = k * beta
            egc = jnp.exp(gc_col)
            kq = _dot_nt(jnp.concatenate([kb, q], axis=0).astype(BF16), k.astype(BF16))
            L = jnp.where(strict, kq[:C] * decay, 0.0)
            attn = jnp.where(tril, kq[C:] * decay, 0.0)
            rhs = jnp.concatenate([v * beta, kb * egc], axis=1).astype(BF16)
            k_dec = k * jnp.exp(g_last - gc_col)
            wq_s[h, c, C:, :] = (q * egc).astype(BF16)
            ka_s[h, c, :Dh, :] = k_dec.T.astype(BF16)
            ka_s[h, c, Dh:, :] = attn.astype(BF16)
            gl_s[h, c] = jnp.broadcast_to(jnp.exp(g_last), (8, Dh))
            staged.append((h, c, r0, L, rhs))
        tinvs = _unit_lower_inverses([s[3] for s in staged])
        for (h, c, r0, _, rhs), tinv in zip(staged, tinvs):
            uw = _dot(tinv.astype(BF16), rhs)
            u_s[h, pl.ds(r0, C), :] = uw[:, :Dh]
            wq_s[h, c, :C, :] = uw[:, Dh:].astype(BF16)

    def recur(i):
        for cc in range(GDN_PREP_CHUNKS):
            c = i * GDN_PREP_CHUNKS + cc
            r0 = c * C if isinstance(c, int) else pl.multiple_of(c * C, C)
            for h in range(HB):
                S = state_s[h]
                r = _dot(wq_s[h, c], S.astype(BF16))
                v_new = u_s[h, pl.ds(r0, C), :] - r[:C]
                r2 = _dot(ka_s[h, c], v_new.astype(BF16))
                state_s[h] = S * gl_s[h, c][0:1, :] + r2[:Dh]
                o = r[C:] + r2[Dh:]
                z = z_ref[pl.ds(r0, C), h * Dh:(h + 1) * Dh]
                on = o * lax.rsqrt(jnp.mean(o * o, axis=-1, keepdims=True) + EPS) * og_ref[...]
                o_ref[pl.ds(r0, C), h * Dh:(h + 1) * Dh] = on * _silu(z)

    n_groups = NC // GDN_PREP_CHUNKS
    prepare(0)

    def body(i, carry):
        recur(i)
        prepare(i + 1)
        return carry

    lax.fori_loop(0, n_groups - 1, body, 0)
    recur(n_groups - 1)


def _gdn_call(proj, ba, conv_w, a_log, dt_bias, o_gain):
    B, T, _ = proj.shape
    H, Dh, HB, TB, C = GDN_HEADS, GDN_HEAD_DIM, GDN_HB, GDN_TB, CHUNK
    G = H // HB
    Wb = HB * Dh
    NC = TB // C
    pad = lambda vec: jnp.zeros((1, LANES), F32).at[0, H:2 * H].set(vec)
    col = lambda off: pl.BlockSpec((None, TB, Wb), lambda b, g, t: (b, t, off + g))
    cw = lambda off: pl.BlockSpec((CONV_WIDTH, Wb), lambda b, g, t: (0, off + g))
    row = pl.BlockSpec((1, LANES), lambda b, g, t: (0, 0))
    return pl.pallas_call(
        _gdn_kernel,
        grid=(B, G, T // TB),
        in_specs=[col(0), col(G), col(2 * G), col(3 * G),
                  pl.BlockSpec((None, TB, LANES), lambda b, g, t: (b, t, 0)),
                  cw(0), cw(G), cw(2 * G), row, row, row],
        out_specs=pl.BlockSpec((None, TB, Wb), lambda b, g, t: (b, t, g)),
        out_shape=jax.ShapeDtypeStruct((B, T, H * Dh), F32),
        scratch_shapes=[pltpu.VMEM((HB, Dh, Dh), F32),
                        pltpu.VMEM((3, TB + HALO, Wb), F32),
                        pltpu.VMEM((HB, TB, Dh), F32),
                        pltpu.VMEM((HB, TB, Dh), F32),
                        pltpu.VMEM((HB, TB, Dh), F32),
                        pltpu.VMEM((HB, TB, 1), F32),
                        pltpu.VMEM((HB, TB, 1), F32),
                        pltpu.VMEM((HB, TB, Dh), F32),
                        pltpu.VMEM((HB, NC, 2 * C, Dh), BF16),
                        pltpu.VMEM((HB, NC, Dh + C, C), BF16),
                        pltpu.VMEM((HB, NC, 8, Dh), F32)],
        compiler_params=_params(3),
        name="gated_delta",
    )(proj, proj, proj, proj, ba, conv_w, conv_w, conv_w,
      pad(a_log), pad(dt_bias), o_gain.reshape(1, Dh))


SB_Q_PER_STEP = 2
SB_WINDOW = 2 * Q_BLOCK
LOG_TINY_F32 = -126.0 * 0.6931471805599453


def _sb_scores(qm2, kw, mask):
    l = _dot_nt(qm2, kw)
    t = jnp.log1p(jnp.exp(-jnp.abs(l)))
    log_keep = -jnp.maximum(l, 0.0) - t
    if mask is not None:
        log_keep = jnp.where(mask, log_keep, 0.0)
    return l, t, log_keep


def _sb_weights(l, t, after, mask):
    w = jnp.exp(jnp.minimum(l, 0.0) - t + after)
    if mask is not None:
        w = jnp.where(mask, w, 0.0)
    return w.astype(BF16)


def _sb_kernel(q_ref, k_ref, v_ref, z_ref, qg_ref, o_ref):
    step = pl.program_id(2)
    QB, Dh, WK = Q_BLOCK, SB_HEAD_DIM, SB_WINDOW
    R = 2 * QB
    ri = lax.broadcasted_iota(jnp.int32, (R, WK), 0) & (QB - 1)
    ci = lax.broadcasted_iota(jnp.int32, (R, WK), 1)
    rr = lax.broadcasted_iota(jnp.int32, (WK, WK), 0)
    cc = lax.broadcasted_iota(jnp.int32, (WK, WK), 1)
    later = jnp.where(rr > cc, 1.0, 0.0).astype(BF16)
    later_blk = later[:QB, :QB]
    low = lax.broadcasted_iota(jnp.int32, (QB, LANES), 1) < Dh

    blocks = []
    for qq in range(SB_Q_PER_STEP):
        qi = step * SB_Q_PER_STEP + qq
        q2 = q_ref[qq * QB:(qq + 1) * QB, :]
        sq = q2 * q2
        s_lo = jnp.sum(jnp.where(low, sq, 0.0), axis=-1, keepdims=True)
        s_hi = jnp.sum(jnp.where(low, 0.0, sq), axis=-1, keepdims=True)
        ms = jnp.where(low, s_lo, s_hi) * (1.0 / Dh)
        qn = q2 * lax.rsqrt(ms + EPS) * qg_ref[...] * (Dh ** -0.5)
        qm2 = jnp.concatenate([jnp.where(low, qn, 0.0), jnp.where(low, 0.0, qn)], axis=0).astype(BF16)
        w0 = jnp.maximum(qi - 1, 0)
        k0 = pl.multiple_of(w0 * QB, QB)
        mask = (w0 * QB + ci) < (qi * QB + ri)
        l, t, log_keep = _sb_scores(qm2, k_ref[pl.ds(k0, WK), :], mask)
        blocks.append((qm2, w0, k0, mask, l, t, log_keep))

    parts = [p for blk in blocks for p in _split2(blk[6])]
    sums = _dot(jnp.concatenate(parts, axis=0), later)

    def cond(c):
        j, _, run = c
        return jnp.logical_and(j >= 0, jnp.max(run) >= LOG_TINY_F32)

    heads = []
    for qq, (qm2, w0, k0, mask, l, t, log_keep) in enumerate(blocks):
        after = sums[2 * qq * R:(2 * qq + 1) * R] + sums[(2 * qq + 1) * R:(2 * qq + 2) * R]
        acc = _dot(_sb_weights(l, t, after, mask), v_ref[pl.ds(k0, WK), :])
        heads.append((acc, jnp.sum(log_keep, axis=-1, keepdims=True)))

    for qq, (qm2, w0, k0, mask, l, t, log_keep) in enumerate(blocks):
        acc, run = heads[qq]

        def body(c, qm2=qm2):
            j, acc, run = c
            r0 = pl.multiple_of(j * QB, QB)
            l, t, log_keep = _sb_scores(qm2, k_ref[pl.ds(r0, QB), :], None)
            hi, mid = _split2(log_keep)
            after = _dot(hi, later_blk) + _dot(mid, later_blk) + run
            acc = acc + _dot(_sb_weights(l, t, after, None), v_ref[pl.ds(r0, QB), :])
            return j - 1, acc, run + jnp.sum(log_keep, axis=-1, keepdims=True)

        acc = lax.while_loop(cond, body, (w0 - 1, acc, run))[1]
        rows = slice(qq * QB, (qq + 1) * QB)
        o_ref[rows, :] = jnp.where(low, acc[:QB], acc[QB:]) * _silu(z_ref[rows, :])


def _sb_call(proj, kv, q_gain):
    B, T, W2 = proj.shape
    W = W2 // 2
    nb = W // LANES
    tq = SB_Q_PER_STEP * Q_BLOCK
    qg = jnp.tile(q_gain, LANES // SB_HEAD_DIM).reshape(1, LANES)
    return pl.pallas_call(
        _sb_kernel,
        grid=(B, nb, T // tq),
        in_specs=[pl.BlockSpec((None, tq, LANES), lambda b, p, i: (b, i, p)),
                  pl.BlockSpec((None, T, LANES), lambda b, p, i: (b, 0, p)),
                  pl.BlockSpec((None, T, LANES), lambda b, p, i: (b, 0, nb + p)),
                  pl.BlockSpec((None, tq, LANES), lambda b, p, i: (b, i, nb + p)),
                  pl.BlockSpec((1, LANES), lambda b, p, i: (0, 0))],
        out_specs=pl.BlockSpec((None, tq, LANES), lambda b, p, i: (b, i, p)),
        out_shape=jax.ShapeDtypeStruct((B, T, W), F32),
        compiler_params=_params(3),
        name="stick_breaking",
    )(proj, kv, kv, proj, qg)


def kernel(x, c, norm_g, w_ada, b_ada, w_in_a, conv_w_a, a_log_a, dt_bias_a, o_gain_a, w_out_a,
           kv_norm_g, w_ada_kv, b_ada_kv, w_kv, k_gain, w_in_b, q_gain_b, w_out_b):
    B, T, D = x.shape
    depth = w_ada.shape[0]
    n_a = w_in_a.shape[0]
    gw = GDN_HEADS * GDN_HEAD_DIM

    mod4 = _mod_call(c, w_ada, b_ada).reshape(depth, B, 1, 3 * D)
    modkv4 = _mod_call(c, w_ada_kv[None], b_ada_kv[None]).reshape(1, B, 1, 2 * D)

    kv = None
    for layer in range(depth):
        if layer < n_a:
            i = layer
            w_main = w_in_a[i, :, :4 * gw].astype(BF16)
            w_tail = jnp.pad(w_in_a[i, :, 4 * gw:], ((0, 0), (0, LANES - 2 * GDN_HEADS))).astype(BF16)
            proj, ba = _norm_proj_call(_norm_proj_tail_kernel, x, norm_g[layer], mod4, layer,
                                       [w_main, w_tail], name="gdn_in_proj")
            o = _gdn_call(proj, ba, conv_w_a[i], a_log_a[i], dt_bias_a[i], o_gain_a[i])
            x = _outproj_call(o, w_out_a[i].astype(BF16), x, mod4, layer)
        else:
            if kv is None:
                kg = jnp.tile(k_gain, LANES // SB_HEAD_DIM).reshape(1, LANES)
                (kv,) = _norm_proj_call(_norm_kv_kernel, x, kv_norm_g, modkv4, 0,
                                        [w_kv.astype(BF16)], extra=(kg,), name="shared_kv",
                                        out_dtype=BF16)
            i = layer - n_a
            (proj,) = _norm_proj_call(_norm_proj_kernel, x, norm_g[layer], mod4, layer,
                                      [w_in_b[i].astype(BF16)], name="sb_in_proj")
            o = _sb_call(proj, kv, q_gain_b[i])
            x = _outproj_call(o, w_out_b[i].astype(BF16), x, mod4, layer)
    return x
```

```python
import functools

import jax
import jax.numpy as jnp
from jax import lax
from jax.experimental import pallas as pl
from jax.experimental.pallas import tpu as pltpu

F32 = jnp.float32
BF16 = jnp.bfloat16
HI = lax.Precision.HIGHEST

EPS = 1e-6
CHUNK = 64
GDN_HEADS = 8
GDN_HEAD_DIM = 128
CONV_WIDTH = 4
SB_HEADS = 16
SB_HEAD_DIM = 64
Q_BLOCK = 128
LANES = 128
VMEM_LIMIT = 48 * 1024 * 1024
ROW_TILE = 256
N_CHUNK = 512


def _params(n_axes):
    return pltpu.CompilerParams(dimension_semantics=("arbitrary",) * n_axes,
                                vmem_limit_bytes=VMEM_LIMIT)


def _silu(x):
    return x * jax.nn.sigmoid(x)


def _dot(a, b, precision=None):
    return jnp.dot(a, b, precision=precision, preferred_element_type=F32)


def _dot_nt(a, b, precision=None):
    return lax.dot_general(a, b, (((1,), (1,)), ((), ())), precision=precision,
                           preferred_element_type=F32)


def _mod_kernel(c_ref, w_ref, b_ref, o_ref):
    ca = _silu(c_ref[...])
    o_ref[...] = _dot(ca, w_ref[...], HI) + b_ref[...]


def _mod_call(c, w, b):
    L, D, N = w.shape
    B = c.shape[0]
    tn = 1024
    return pl.pallas_call(
        _mod_kernel,
        grid=(L, N // tn),
        in_specs=[pl.BlockSpec((B, D), lambda l, j: (0, 0)),
                  pl.BlockSpec((None, D, tn), lambda l, j: (l, 0, j)),
                  pl.BlockSpec((None, 1, tn), lambda l, j: (l, 0, j))],
        out_specs=pl.BlockSpec((None, B, tn), lambda l, j: (l, 0, j)),
        out_shape=jax.ShapeDtypeStruct((L, B, N), F32),
        compiler_params=_params(2),
        name="adaln_mod",
    )(c, w, b.reshape(L, 1, N))


def _modnorm(x_ref, g_ref, sh_ref, sc_ref):
    x = x_ref[...]
    n = x * lax.rsqrt(jnp.mean(x * x, axis=-1, keepdims=True) + EPS) * g_ref[...]
    return (n * (1.0 + sc_ref[...]) + sh_ref[...]).astype(BF16)


def _proj_cols(hb, w_ref, o_ref):
    n = w_ref.shape[1]
    for j in range(0, n, N_CHUNK):
        o_ref[:, j:j + N_CHUNK] = _dot(hb, w_ref[:, j:j + N_CHUNK])


def _norm_proj_kernel(x_ref, g_ref, sh_ref, sc_ref, w_ref, o_ref):
    _proj_cols(_modnorm(x_ref, g_ref, sh_ref, sc_ref), w_ref, o_ref)


def _norm_proj_tail_kernel(x_ref, g_ref, sh_ref, sc_ref, w_ref, wt_ref, o_ref, ot_ref):
    hb = _modnorm(x_ref, g_ref, sh_ref, sc_ref)
    _proj_cols(hb, w_ref, o_ref)
    ot_ref[...] = _dot(hb, wt_ref[...])


def _norm_kv_kernel(x_ref, g_ref, sh_ref, sc_ref, w_ref, kg_ref, o_ref):
    hb = _modnorm(x_ref, g_ref, sh_ref, sc_ref)
    n = w_ref.shape[1]
    half = n // 2
    tm = x_ref.shape[0]
    low = lax.broadcasted_iota(jnp.int32, (tm, LANES), 1) < SB_HEAD_DIM
    for j in range(0, half, LANES):
        kk = _dot(hb, w_ref[:, j:j + LANES])
        sq = kk * kk
        s_lo = jnp.sum(jnp.where(low, sq, 0.0), axis=-1, keepdims=True)
        s_hi = jnp.sum(jnp.where(low, 0.0, sq), axis=-1, keepdims=True)
        ms = jnp.where(low, s_lo, s_hi) * (1.0 / SB_HEAD_DIM)
        o_ref[:, j:j + LANES] = (kk * lax.rsqrt(ms + EPS) * kg_ref[...]).astype(o_ref.dtype)
    for j in range(half, n, N_CHUNK):
        o_ref[:, j:j + N_CHUNK] = _dot(hb, w_ref[:, j:j + N_CHUNK]).astype(o_ref.dtype)


def _norm_proj_call(kernel_fn, x, gain, mod4, layer, w_list, extra=(), name="norm_proj",
                    out_dtype=F32):
    B, T, D = x.shape
    tm = ROW_TILE
    in_specs = [pl.BlockSpec((None, tm, D), lambda b, i: (b, i, 0)),
                pl.BlockSpec((1, D), lambda b, i: (0, 0)),
                pl.BlockSpec((None, None, 1, D), lambda b, i: (layer, b, 0, 0)),
                pl.BlockSpec((None, None, 1, D), lambda b, i: (layer, b, 0, 1))]
    out_specs, out_shape = [], []
    for w in w_list:
        n = w.shape[1]
        in_specs.append(pl.BlockSpec((D, n), lambda b, i: (0, 0)))
        out_specs.append(pl.BlockSpec((None, tm, n), lambda b, i: (b, i, 0)))
        out_shape.append(jax.ShapeDtypeStruct((B, T, n), out_dtype))
    for e in extra:
        in_specs.append(pl.BlockSpec(e.shape, lambda b, i: (0, 0)))
    return pl.pallas_call(
        kernel_fn,
        grid=(B, T // tm),
        in_specs=in_specs,
        out_specs=out_specs,
        out_shape=out_shape,
        compiler_params=_params(2),
        name=name,
    )(x, gain.reshape(1, D), mod4, mod4, *w_list, *extra)


def _outproj_kernel(o_ref, w_ref, x_ref, gate_ref, xo_ref):
    y = _dot(o_ref[...].astype(BF16), w_ref[...])
    xo_ref[...] = x_ref[...] + gate_ref[...] * y


def _outproj_call(o, w, x, mod4, layer):
    B, T, D = x.shape
    W = o.shape[-1]
    tm = ROW_TILE
    return pl.pallas_call(
        _outproj_kernel,
        grid=(B, T // tm),
        in_specs=[pl.BlockSpec((None, tm, W), lambda b, i: (b, i, 0)),
                  pl.BlockSpec((W, D), lambda b, i: (0, 0)),
                  pl.BlockSpec((None, tm, D), lambda b, i: (b, i, 0)),
                  pl.BlockSpec((None, None, 1, D), lambda b, i: (layer, b, 0, 2))],
        out_specs=pl.BlockSpec((None, tm, D), lambda b, i: (b, i, 0)),
        out_shape=jax.ShapeDtypeStruct((B, T, D), F32),
        compiler_params=_params(2),
        name="outproj_residual",
    )(o, w, x, mod4)


GDN_HB = 4
GDN_TB = 512
GDN_PREP_CHUNKS = 2
HALO = 8


def _split2(x):
    hi = x.astype(BF16)
    return hi, (x - hi.astype(F32)).astype(BF16)


def _dot_split(lhs_rows, p):
    n = p.shape[0]
    m = len(lhs_rows)
    his, los = zip(*[_split2(a) for a in lhs_rows])
    ph, plo = _split2(p)
    r1 = _dot(jnp.concatenate(his + los, axis=0), ph)
    r2 = _dot(jnp.concatenate(his, axis=0), plo)
    return [r1[i * n:(i + 1) * n] + r1[(m + i) * n:(m + i + 1) * n] + r2[i * n:(i + 1) * n]
            for i in range(m)]


def _unit_lower_inverses(Ls):
    n = Ls[0].shape[0]
    ri = lax.broadcasted_iota(jnp.int32, (n, n), 0)
    ci = lax.broadcasted_iota(jnp.int32, (n, n), 1)
    eye = jnp.where(ri == ci, 1.0, 0.0)
    Xs = [eye - L for L in Ls]
    Ps = [_dot_split([L], L)[0] for L in Ls]
    k = 4
    while k < n:
        outs = [_dot_split([X, P], P) for X, P in zip(Xs, Ps)]
        Xs = [X + o[0] for X, o in zip(Xs, outs)]
        Ps = [o[1] for o in outs]
        k *= 2
    return [X + _dot_split([X], P)[0] for X, P in zip(Xs, Ps)]


def _gdn_kernel(q_ref, k_ref, v_ref, z_ref, ba_ref, cwq_ref, cwk_ref, cwv_ref,
                alog_ref, dtb_ref, og_ref, o_ref,
                state_s, xe_s, qn_s, kn_s, vn_s, beta_s, g_s, u_s, wq_s, ka_s, gl_s):
    hg = pl.program_id(1)
    tb = pl.program_id(2)
    TB, W = q_ref.shape
    HB, Dh, C = GDN_HB, GDN_HEAD_DIM, CHUNK
    NC = TB // C

    @pl.when(tb == 0)
    def _():
        state_s[...] = jnp.zeros_like(state_s)
        xe_s[:, :HALO, :] = jnp.zeros((3, HALO, W), F32)

    for idx, (x_ref, cw_ref, dst) in enumerate(((q_ref, cwq_ref, qn_s), (k_ref, cwk_ref, kn_s),
                                                (v_ref, cwv_ref, vn_s))):
        xe_s[idx, HALO:, :] = x_ref[...]
        cw = cw_ref[...]
        acc = x_ref[...] * cw[CONV_WIDTH - 1:CONV_WIDTH, :]
        for s in range(1, CONV_WIDTH):
            acc = acc + xe_s[idx, HALO - s:HALO - s + TB, :] * cw[CONV_WIDTH - 1 - s:CONV_WIDTH - s, :]
        xe_s[idx, :HALO, :] = xe_s[idx, TB:, :]
        y = _silu(acc)
        for h in range(HB):
            yh = y[:, h * Dh:(h + 1) * Dh]
            if idx == 0:
                yh = yh * lax.rsqrt(jnp.sum(yh * yh, axis=-1, keepdims=True) + EPS) * (Dh ** -0.5)
            elif idx == 1:
                yh = yh * lax.rsqrt(jnp.sum(yh * yh, axis=-1, keepdims=True) + EPS)
            dst[h] = yh

    lane = lax.broadcasted_iota(jnp.int32, (TB, LANES), 1)
    ba = ba_ref[...]
    beta_all = jax.nn.sigmoid(ba)
    a = ba + dtb_ref[...]
    g_all = -jnp.exp(alog_ref[...]) * (jnp.maximum(a, 0.0) + jnp.log1p(jnp.exp(-jnp.abs(a))))
    for h in range(HB):
        head = hg * HB + h
        beta_s[h] = jnp.sum(jnp.where(lane == head, beta_all, 0.0), axis=-1, keepdims=True)
        g_s[h] = jnp.sum(jnp.where(lane == head + GDN_HEADS, g_all, 0.0), axis=-1, keepdims=True)

    ri = lax.broadcasted_iota(jnp.int32, (C, C), 0)
    ci = lax.broadcasted_iota(jnp.int32, (C, C), 1)
    tril = ri >= ci
    strict = ri > ci
    eye = ri == ci

    def to_row(col):
        return jnp.sum(jnp.where(eye, jnp.broadcast_to(col, (C, C)), 0.0), axis=0, keepdims=True)

    def prepare(i):
        chains = [(h, i * GDN_PREP_CHUNKS + cc) for cc in range(GDN_PREP_CHUNKS) for h in range(HB)]
        staged = []
        for h, c in chains:
            r0 = c * C if isinstance(c, int) else pl.multiple_of(c * C, C)
            q = qn_s[h, pl.ds(r0, C), :]
            k = kn_s[h, pl.ds(r0, C), :]
            v = vn_s[h, pl.ds(r0, C), :]
            beta = beta_s[h, pl.ds(r0, C), :]
            g_row = to_row(g_s[h, pl.ds(r0, C), :])
            gc_col = jnp.sum(jnp.where(tril, jnp.broadcast_to(g_row, (C, C)), 0.0),
                             axis=1, keepdims=True)
            gc_row = to_row(gc_col)
            g_last = gc_col[C - 1:C, :]
            decay = jnp.where(tril, jnp.exp(jnp.where(tril, gc_col - gc_row, 0.0)), 0.0)
            kb = k * beta
            egc = jnp.exp(gc_col)
            kq = _dot_nt(jnp.concatenate([kb, q], axis=0).astype(BF16), k.astype(BF16))
            L = jnp.where(strict, kq[:C] * decay, 0.0)
            attn = jnp.where(tril, kq[C:] * decay, 0.0)
            rhs = jnp.concatenate([v * beta, kb * egc], axis=1).astype(BF16)
            k_dec = k * jnp.exp(g_last - gc_col)
            wq_s[h, c, C:, :] = (q * egc).astype(BF16)
            ka_s[h, c, :Dh, :] = k_dec.T.astype(BF16)
            ka_s[h, c, Dh:, :] = attn.astype(BF16)
            gl_s[h, c] = jnp.broadcast_to(jnp.exp(g_last), (8, Dh))
            staged.append((h, c, r0, L, rhs))
        tinvs = _unit_lower_inverses([s[3] for s in staged])
        for (h, c, r0, _, rhs), tinv in zip(staged, tinvs):
            uw = _dot(tinv.astype(BF16), rhs)
            u_s[h, pl.ds(r0, C), :] = uw[:, :Dh]
            wq_s[h, c, :C, :] = uw[:, Dh:].astype(BF16)

    def recur(i):
        for cc in range(GDN_PREP_CHUNKS):
            c = i * GDN_PREP_CHUNKS + cc
            r0 = c * C if isinstance(c, int) else pl.multiple_of(c * C, C)
            for h in range(HB):
                S = state_s[h]
                r = _dot(wq_s[h, c], S.astype(BF16))
                v_new = u_s[h, pl.ds(r0, C), :] - r[:C]
                r2 = _dot(ka_s[h, c], v_new.astype(BF16))
                state_s[h] = S * gl_s[h, c][0:1, :] + r2[:Dh]
                o = r[C:] + r2[Dh:]
                z = z_ref[pl.ds(r0, C), h * Dh:(h + 1) * Dh]
                on = o * lax.rsqrt(jnp.mean(o * o, axis=-1, keepdims=True) + EPS) * og_ref[...]
                o_ref[pl.ds(r0, C), h * Dh:(h + 1) * Dh] = on * _silu(z)

    n_groups = NC // GDN_PREP_CHUNKS
    prepare(0)

    def body(i, carry):
        recur(i)
        prepare(i + 1)
        return carry

    lax.fori_loop(0, n_groups - 1, body, 0)
    recur(n_groups - 1)


def _gdn_call(proj, ba, conv_w, a_log, dt_bias, o_gain):
    B, T, _ = proj.shape
    H, Dh, HB, TB, C = GDN_HEADS, GDN_HEAD_DIM, GDN_HB, GDN_TB, CHUNK
    G = H // HB
    Wb = HB * Dh
    NC = TB // C
    pad = lambda vec: jnp.zeros((1, LANES), F32).at[0, H:2 * H].set(vec)
    col = lambda off: pl.BlockSpec((None, TB, Wb), lambda b, g, t: (b, t, off + g))
    cw = lambda off: pl.BlockSpec((CONV_WIDTH, Wb), lambda b, g, t: (0, off + g))
    row = pl.BlockSpec((1, LANES), lambda b, g, t: (0, 0))
    return pl.pallas_call(
        _gdn_kernel,
        grid=(B, G, T // TB),
        in_specs=[col(0), col(G), col(2 * G), col(3 * G),
                  pl.BlockSpec((None, TB, LANES), lambda b, g, t: (b, t, 0)),
                  cw(0), cw(G), cw(2 * G), row, row, row],
        out_specs=pl.BlockSpec((None, TB, Wb), lambda b, g, t: (b, t, g)),
        out_shape=jax.ShapeDtypeStruct((B, T, H * Dh), F32),
        scratch_shapes=[pltpu.VMEM((HB, Dh, Dh), F32),
                        pltpu.VMEM((3, TB + HALO, Wb), F32),
                        pltpu.VMEM((HB, TB, Dh), F32),
                        pltpu.VMEM((HB, TB, Dh), F32),
                        pltpu.VMEM((HB, TB, Dh), F32),
                        pltpu.VMEM((HB, TB, 1), F32),
                        pltpu.VMEM((HB, TB, 1), F32),
                        pltpu.VMEM((HB, TB, Dh), F32),
                        pltpu.VMEM((HB, NC, 2 * C, Dh), BF16),
                        pltpu.VMEM((HB, NC, Dh + C, C), BF16),
                        pltpu.VMEM((HB, NC, 8, Dh), F32)],
        compiler_params=_params(3),
        name="gated_delta",
    )(proj, proj, proj, proj, ba, conv_w, conv_w, conv_w,
      pad(a_log), pad(dt_bias), o_gain.reshape(1, Dh))


SB_Q_PER_STEP = 4
SB_WINDOW = 3 * Q_BLOCK
LOG2_E = 1.4426950408889634
LOG2_TINY_F32 = -126.0
MASKED_LOG2 = -1e30


def _sb_log2_terms(l2):
    neg, pos = jnp.minimum(l2, 0.0), jnp.maximum(l2, 0.0)
    t2 = jnp.log2(1.0 + jnp.exp2(neg - pos))
    return neg - t2, pos + t2


def _suffix_sums(drop, later):
    r = drop.shape[0]
    s = _dot(jnp.concatenate(_split2(drop), axis=0), later)
    return s[:r] + s[r:]


def _sb_kernel(q_ref, k_ref, v_ref, z_ref, qg_ref, later_ref, o_ref):
    step = pl.program_id(2)
    QB, Dh, WK = Q_BLOCK, SB_HEAD_DIM, SB_WINDOW
    NW = WK // QB
    WA = WK - QB
    R = 2 * QB
    diag = (lax.broadcasted_iota(jnp.int32, (R, WK), 1)
            - (lax.broadcasted_iota(jnp.int32, (R, WK), 0) & (QB - 1)))
    low = lax.broadcasted_iota(jnp.int32, (QB, LANES), 1) < Dh
    first_q = step * SB_Q_PER_STEP

    pre, qms = [], []
    for qq in range(SB_Q_PER_STEP):
        qi = first_q + qq
        q2 = q_ref[qq * QB:(qq + 1) * QB, :]
        sq = q2 * q2
        s_lo = jnp.sum(jnp.where(low, sq, 0.0), axis=-1, keepdims=True)
        s_hi = jnp.sum(jnp.where(low, 0.0, sq), axis=-1, keepdims=True)
        ms = jnp.where(low, s_lo, s_hi) * (1.0 / Dh)
        qn = q2 * lax.rsqrt(ms + EPS) * qg_ref[...] * (LOG2_E * Dh ** -0.5)
        qm2 = jnp.concatenate([jnp.where(low, qn, 0.0), jnp.where(low, 0.0, qn)], axis=0).astype(BF16)
        qms.append(qm2)
        w0 = jnp.maximum(qi - (NW - 1), 0)
        k0 = pl.multiple_of(w0 * QB, QB)
        pre.append((k0, qi - w0, _dot_nt(qm2, k_ref[pl.ds(k0, WK), :])))

    mid = []
    for k0, dblk, l2 in pre:
        mask = diag < dblk * QB
        log2_sig, drop = _sb_log2_terms(l2)
        log2_sig = jnp.where(mask, log2_sig, MASKED_LOG2)
        drop = jnp.where(mask, drop, 0.0)
        drop_a, drop_b = drop[:, :WA], drop[:, WA:]
        mid.append((k0, log2_sig, _suffix_sums(drop_a, later_ref[...]),
                    _suffix_sums(drop_b, later_ref[:QB, :QB]),
                    jnp.sum(drop_a, axis=-1, keepdims=True), jnp.sum(drop_b, axis=-1, keepdims=True)))

    accs, runs = [], []
    for k0, log2_sig, after_a, after_b, tot_a, tot_b in mid:
        after = jnp.concatenate([after_a + tot_b, after_b], axis=1)
        w = jnp.exp2(log2_sig - after).astype(BF16)
        accs.append(_dot(w, v_ref[pl.ds(k0, WK), :]))
        runs.append(tot_a + tot_b)

    def unfinished(run):
        return jnp.min(run) <= -LOG2_TINY_F32

    def earlier_blocks():
        out = []
        for qq in range(SB_Q_PER_STEP):
            def cond(c):
                j, _, run = c
                return jnp.logical_and(j >= 0, unfinished(run))

            def body(c, qm2=qms[qq]):
                j, acc, run = c
                r0 = pl.multiple_of(j * QB, QB)
                log2_sig, drop = _sb_log2_terms(_dot_nt(qm2, k_ref[pl.ds(r0, QB), :]))
                w = jnp.exp2(log2_sig - (_suffix_sums(drop, later_ref[:QB, :QB]) + run)).astype(BF16)
                acc = acc + _dot(w, v_ref[pl.ds(r0, QB), :])
                return j - 1, acc, run + jnp.sum(drop, axis=-1, keepdims=True)

            out.append(lax.while_loop(cond, body, (first_q + qq - NW, accs[qq], runs[qq]))[1])
        return out

    slowest = functools.reduce(jnp.minimum, runs)
    any_left = jnp.logical_and(first_q + SB_Q_PER_STEP - 1 >= NW, unfinished(slowest))
    accs = lax.cond(any_left, earlier_blocks, lambda: accs)
    for qq, acc in enumerate(accs):
        rows = slice(qq * QB, (qq + 1) * QB)
        o_ref[rows, :] = jnp.where(low, acc[:QB], acc[QB:]) * _silu(z_ref[rows, :])


def _sb_call(proj, kv, q_gain):
    B, T, W2 = proj.shape
    W = W2 // 2
    nb = W // LANES
    tq = SB_Q_PER_STEP * Q_BLOCK
    wa = SB_WINDOW - Q_BLOCK
    qg = jnp.tile(q_gain, LANES // SB_HEAD_DIM).reshape(1, LANES)
    later = jnp.tri(wa, k=-1, dtype=BF16)
    return pl.pallas_call(
        _sb_kernel,
        grid=(B, nb, T // tq),
        in_specs=[pl.BlockSpec((None, tq, LANES), lambda b, p, i: (b, i, p)),
                  pl.BlockSpec((None, T, LANES), lambda b, p, i: (b, 0, p)),
                  pl.BlockSpec((None, T, LANES), lambda b, p, i: (b, 0, nb + p)),
                  pl.BlockSpec((None, tq, LANES), lambda b, p, i: (b, i, nb + p)),
                  pl.BlockSpec((1, LANES), lambda b, p, i: (0, 0)),
                  pl.BlockSpec((wa, wa), lambda b, p, i: (0, 0))],
        out_specs=pl.BlockSpec((None, tq, LANES), lambda b, p, i: (b, i, p)),
        out_shape=jax.ShapeDtypeStruct((B, T, W), F32),
        compiler_params=_params(3),
        name="stick_breaking",
    )(proj, kv, kv, proj, qg, later)


def kernel(x, c, norm_g, w_ada, b_ada, w_in_a, conv_w_a, a_log_a, dt_bias_a, o_gain_a, w_out_a,
           kv_norm_g, w_ada_kv, b_ada_kv, w_kv, k_gain, w_in_b, q_gain_b, w_out_b):
    B, T, D = x.shape
    depth = w_ada.shape[0]
    n_a = w_in_a.shape[0]
    gw = GDN_HEADS * GDN_HEAD_DIM

    mod4 = _mod_call(c, w_ada, b_ada).reshape(depth, B, 1, 3 * D)
    modkv4 = _mod_call(c, w_ada_kv[None], b_ada_kv[None]).reshape(1, B, 1, 2 * D)

    kv = None
    for layer in range(depth):
        if layer < n_a:
            i = layer
            w_main = w_in_a[i, :, :4 * gw].astype(BF16)
            w_tail = jnp.pad(w_in_a[i, :, 4 * gw:], ((0, 0), (0, LANES - 2 * GDN_HEADS))).astype(BF16)
            proj, ba = _norm_proj_call(_norm_proj_tail_kernel, x, norm_g[layer], mod4, layer,
                                       [w_main, w_tail], name="gdn_in_proj")
            o = _gdn_call(proj, ba, conv_w_a[i], a_log_a[i], dt_bias_a[i], o_gain_a[i])
            x = _outproj_call(o, w_out_a[i].astype(BF16), x, mod4, layer)
        else:
            if kv is None:
                kg = jnp.tile(k_gain, LANES // SB_HEAD_DIM).reshape(1, LANES)
                (kv,) = _norm_proj_call(_norm_kv_kernel, x, kv_norm_g, modkv4, 0,
                                        [w_kv.astype(BF16)], extra=(kg,), name="shared_kv",
                                        out_dtype=BF16)
            i = layer - n_a
            (proj,) = _norm_proj_call(_norm_proj_kernel, x, norm_g[layer], mod4, layer,
                                      [w_in_b[i].astype(BF16)], name="sb_in_proj")
            o = _sb_call(proj, kv, q_gain_b[i])
            x = _outproj_call(o, w_out_b[i].astype(BF16), x, mod4, layer)
    return x
```

```python
import functools

import jax
import jax.numpy as jnp
from jax import lax
from jax.experimental import pallas as pl
from jax.experimental.pallas import tpu as pltpu

F32 = jnp.float32
BF16 = jnp.bfloat16
HI = lax.Precision.HIGHEST

EPS = 1e-6
CHUNK = 64
GDN_HEADS = 8
GDN_HEAD_DIM = 128
CONV_WIDTH = 4
SB_HEADS = 16
SB_HEAD_DIM = 64
Q_BLOCK = 128
LANES = 128
VMEM_LIMIT = 48 * 1024 * 1024
ROW_TILE = 256
N_CHUNK = 512
HALO = 8


def _params(n_axes):
    return pltpu.CompilerParams(dimension_semantics=("arbitrary",) * n_axes,
                                vmem_limit_bytes=VMEM_LIMIT)


def _silu(x):
    return x * jax.nn.sigmoid(x)


def _dot(a, b, precision=None):
    return jnp.dot(a, b, precision=precision, preferred_element_type=F32)


def _dot_nt(a, b, precision=None):
    return lax.dot_general(a, b, (((1,), (1,)), ((), ())), precision=precision,
                           preferred_element_type=F32)


def _mod_kernel(c_ref, w_ref, b_ref, o_ref):
    ca = _silu(c_ref[...])
    o_ref[...] = _dot(ca, w_ref[...], HI) + b_ref[...]


def _mod_call(c, w, b):
    L, D, N = w.shape
    B = c.shape[0]
    tn = 1024
    return pl.pallas_call(
        _mod_kernel,
        grid=(L, N // tn),
        in_specs=[pl.BlockSpec((B, D), lambda l, j: (0, 0)),
                  pl.BlockSpec((None, D, tn), lambda l, j: (l, 0, j)),
                  pl.BlockSpec((None, 1, tn), lambda l, j: (l, 0, j))],
        out_specs=pl.BlockSpec((None, B, tn), lambda l, j: (l, 0, j)),
        out_shape=jax.ShapeDtypeStruct((L, B, N), F32),
        compiler_params=_params(2),
        name="adaln_mod",
    )(c, w, b.reshape(L, 1, N))


def _modnorm(x_ref, g_ref, sh_ref, sc_ref):
    x = x_ref[...]
    n = x * lax.rsqrt(jnp.mean(x * x, axis=-1, keepdims=True) + EPS) * g_ref[...]
    return (n * (1.0 + sc_ref[...]) + sh_ref[...]).astype(BF16)


def _proj_cols(hb, w_ref, o_ref):
    n = w_ref.shape[1]
    for j in range(0, n, N_CHUNK):
        o_ref[:, j:j + N_CHUNK] = _dot(hb, w_ref[:, j:j + N_CHUNK])


def _norm_proj_kernel(x_ref, g_ref, sh_ref, sc_ref, w_ref, o_ref):
    _proj_cols(_modnorm(x_ref, g_ref, sh_ref, sc_ref), w_ref, o_ref)


def _gdn_in_proj_kernel(x_ref, g_ref, sh_ref, sc_ref, w_ref, wt_ref, cw_ref, alog_ref, dtb_ref,
                        o_ref, bg_ref, xe_s):
    tm = x_ref.shape[0]
    Dh = GDN_HEAD_DIM
    w3 = cw_ref.shape[1]
    wd = w3 // 3

    @pl.when(pl.program_id(1) == 0)
    def _():
        xe_s[:HALO, :] = jnp.zeros((HALO, w3), F32)

    hb = _modnorm(x_ref, g_ref, sh_ref, sc_ref)
    for j in range(0, w3, N_CHUNK):
        cols = slice(j, j + N_CHUNK)
        raw = _dot(hb, w_ref[:, cols])
        xe_s[HALO:, cols] = raw
        cw = cw_ref[:, cols]
        acc = raw * cw[CONV_WIDTH - 1:CONV_WIDTH, :]
        for s in range(1, CONV_WIDTH):
            acc = acc + xe_s[HALO - s:HALO - s + tm, cols] * cw[CONV_WIDTH - 1 - s:CONV_WIDTH - s, :]
        xe_s[:HALO, cols] = raw[tm - HALO:]
        y = _silu(acc)
        if j < 2 * wd:
            scale = Dh ** -0.5 if j < wd else 1.0
            for h0 in range(0, N_CHUNK, Dh):
                yh = y[:, h0:h0 + Dh]
                inv = lax.rsqrt(jnp.sum(yh * yh, axis=-1, keepdims=True) + EPS) * scale
                o_ref[:, j + h0:j + h0 + Dh] = yh * inv
        else:
            o_ref[:, cols] = y
    for j in range(w3, w_ref.shape[1], N_CHUNK):
        o_ref[:, j:j + N_CHUNK] = _dot(hb, w_ref[:, j:j + N_CHUNK])
    t = _dot(hb, wt_ref[...])
    a = t + dtb_ref[...]
    g = -jnp.exp(alog_ref[...]) * (jnp.maximum(a, 0.0) + jnp.log1p(jnp.exp(-jnp.abs(a))))
    lane = lax.broadcasted_iota(jnp.int32, t.shape, 1)
    bg_ref[...] = jnp.where(lane < GDN_HEADS, jax.nn.sigmoid(t), g)


def _norm_kv_kernel(x_ref, g_ref, sh_ref, sc_ref, w_ref, kg_ref, o_ref):
    hb = _modnorm(x_ref, g_ref, sh_ref, sc_ref)
    n = w_ref.shape[1]
    half = n // 2
    tm = x_ref.shape[0]
    low = lax.broadcasted_iota(jnp.int32, (tm, LANES), 1) < SB_HEAD_DIM
    for j in range(0, half, LANES):
        kk = _dot(hb, w_ref[:, j:j + LANES])
        sq = kk * kk
        s_lo = jnp.sum(jnp.where(low, sq, 0.0), axis=-1, keepdims=True)
        s_hi = jnp.sum(jnp.where(low, 0.0, sq), axis=-1, keepdims=True)
        ms = jnp.where(low, s_lo, s_hi) * (1.0 / SB_HEAD_DIM)
        o_ref[:, j:j + LANES] = (kk * lax.rsqrt(ms + EPS) * kg_ref[...]).astype(o_ref.dtype)
    for j in range(half, n, N_CHUNK):
        o_ref[:, j:j + N_CHUNK] = _dot(hb, w_ref[:, j:j + N_CHUNK]).astype(o_ref.dtype)


def _norm_proj_call(kernel_fn, x, gain, mod4, layer, w_list, extra=(), name="norm_proj",
                    out_dtype=F32, scratch_shapes=()):
    B, T, D = x.shape
    tm = ROW_TILE
    in_specs = [pl.BlockSpec((None, tm, D), lambda b, i: (b, i, 0)),
                pl.BlockSpec((1, D), lambda b, i: (0, 0)),
                pl.BlockSpec((None, None, 1, D), lambda b, i: (layer, b, 0, 0)),
                pl.BlockSpec((None, None, 1, D), lambda b, i: (layer, b, 0, 1))]
    out_specs, out_shape = [], []
    for w in w_list:
        n = w.shape[1]
        in_specs.append(pl.BlockSpec((D, n), lambda b, i: (0, 0)))
        out_specs.append(pl.BlockSpec((None, tm, n), lambda b, i: (b, i, 0)))
        out_shape.append(jax.ShapeDtypeStruct((B, T, n), out_dtype))
    for e in extra:
        in_specs.append(pl.BlockSpec(e.shape, lambda b, i: (0, 0)))
    return pl.pallas_call(
        kernel_fn,
        grid=(B, T // tm),
        in_specs=in_specs,
        out_specs=out_specs,
        out_shape=out_shape,
        scratch_shapes=list(scratch_shapes),
        compiler_params=_params(2),
        name=name,
    )(x, gain.reshape(1, D), mod4, mod4, *w_list, *extra)


def _outproj_kernel(o_ref, w_ref, x_ref, gate_ref, xo_ref):
    y = _dot(o_ref[...], w_ref[...])
    xo_ref[...] = x_ref[...] + gate_ref[...] * y


def _outproj_call(o, w, x, mod4, layer):
    B, T, D = x.shape
    W = o.shape[-1]
    tm = ROW_TILE
    return pl.pallas_call(
        _outproj_kernel,
        grid=(B, T // tm),
        in_specs=[pl.BlockSpec((None, tm, W), lambda b, i: (b, i, 0)),
                  pl.BlockSpec((W, D), lambda b, i: (0, 0)),
                  pl.BlockSpec((None, tm, D), lambda b, i: (b, i, 0)),
                  pl.BlockSpec((None, None, 1, D), lambda b, i: (layer, b, 0, 2))],
        out_specs=pl.BlockSpec((None, tm, D), lambda b, i: (b, i, 0)),
        out_shape=jax.ShapeDtypeStruct((B, T, D), F32),
        compiler_params=_params(2),
        name="outproj_residual",
    )(o, w, x, mod4)


GDN_HB = 4
GDN_TB = 512
GDN_PREP_CHUNKS = 2


def _split2(x):
    hi = x.astype(BF16)
    return hi, (x - hi.astype(F32)).astype(BF16)


def _dot_split(lhs_rows, p):
    n = p.shape[0]
    m = len(lhs_rows)
    his, los = zip(*[_split2(a) for a in lhs_rows])
    ph, plo = _split2(p)
    r1 = _dot(jnp.concatenate(his + los, axis=0), ph)
    r2 = _dot(jnp.concatenate(his, axis=0), plo)
    return [r1[i * n:(i + 1) * n] + r1[(m + i) * n:(m + i + 1) * n] + r2[i * n:(i + 1) * n]
            for i in range(m)]


def _unit_lower_inverses(Ls):
    n = Ls[0].shape[0]
    ri = lax.broadcasted_iota(jnp.int32, (n, n), 0)
    ci = lax.broadcasted_iota(jnp.int32, (n, n), 1)
    eye = jnp.where(ri == ci, 1.0, 0.0)
    Xs = [eye - L for L in Ls]
    Ps = [_dot_split([L], L)[0] for L in Ls]
    k = 4
    while k < n:
        outs = [_dot_split([X, P], P) for X, P in zip(Xs, Ps)]
        Xs = [X + o[0] for X, o in zip(Xs, outs)]
        Ps = [o[1] for o in outs]
        k *= 2
    return [X + _dot_split([X], P)[0] for X, P in zip(Xs, Ps)]


def _gdn_kernel(q_ref, k_ref, v_ref, z_ref, bg_ref, og_ref, o_ref,
                state_s, beta_s, g_s, u_s, wq_s, ka_s, gl_s):
    hg = pl.program_id(1)
    tb = pl.program_id(2)
    TB = q_ref.shape[0]
    HB, Dh, C = GDN_HB, GDN_HEAD_DIM, CHUNK
    NC = TB // C

    @pl.when(tb == 0)
    def _():
        state_s[...] = jnp.zeros_like(state_s)

    lane = lax.broadcasted_iota(jnp.int32, (TB, LANES), 1)
    bg = bg_ref[...]
    for h in range(HB):
        head = hg * HB + h
        beta_s[h] = jnp.sum(jnp.where(lane == head, bg, 0.0), axis=-1, keepdims=True)
        g_s[h] = jnp.sum(jnp.where(lane == head + GDN_HEADS, bg, 0.0), axis=-1, keepdims=True)

    ri = lax.broadcasted_iota(jnp.int32, (C, C), 0)
    ci = lax.broadcasted_iota(jnp.int32, (C, C), 1)
    tril = ri >= ci
    strict = ri > ci
    eye = ri == ci

    def to_row(col):
        return jnp.sum(jnp.where(eye, jnp.broadcast_to(col, (C, C)), 0.0), axis=0, keepdims=True)

    def prepare(i):
        chains = [(h, i * GDN_PREP_CHUNKS + cc) for cc in range(GDN_PREP_CHUNKS) for h in range(HB)]
        staged = []
        for h, c in chains:
            r0 = c * C if isinstance(c, int) else pl.multiple_of(c * C, C)
            q = q_ref[pl.ds(r0, C), h * Dh:(h + 1) * Dh]
            k = k_ref[pl.ds(r0, C), h * Dh:(h + 1) * Dh]
            v = v_ref[pl.ds(r0, C), h * Dh:(h + 1) * Dh]
            beta = beta_s[h, pl.ds(r0, C), :]
            g_row = to_row(g_s[h, pl.ds(r0, C), :])
            gc_col = jnp.sum(jnp.where(tril, jnp.broadcast_to(g_row, (C, C)), 0.0),
                             axis=1, keepdims=True)
            gc_row = to_row(gc_col)
            g_last = gc_col[C - 1:C, :]
            decay = jnp.where(tril, jnp.exp(jnp.where(tril, gc_col - gc_row, 0.0)), 0.0)
            kb = k * beta
            egc = jnp.exp(gc_col)
            kq = _dot_nt(jnp.concatenate([kb, q], axis=0).astype(BF16), k.astype(BF16))
            L = jnp.where(strict, kq[:C] * decay, 0.0)
            attn = jnp.where(tril, kq[C:] * decay, 0.0)
            rhs = jnp.concatenate([v * beta, kb * egc], axis=1).astype(BF16)
            k_dec = k * jnp.exp(g_last - gc_col)
            wq_s[h, c, C:, :] = (q * egc).astype(BF16)
            ka_s[h, c, :Dh, :] = k_dec.T.astype(BF16)
            ka_s[h, c, Dh:, :] = attn.astype(BF16)
            gl_s[h, c] = jnp.broadcast_to(jnp.exp(g_last), (8, Dh))
            staged.append((h, c, r0, L, rhs))
        tinvs = _unit_lower_inverses([s[3] for s in staged])
        for (h, c, r0, _, rhs), tinv in zip(staged, tinvs):
            uw = _dot(tinv.astype(BF16), rhs)
            u_s[h, pl.ds(r0, C), :] = uw[:, :Dh]
            wq_s[h, c, :C, :] = uw[:, Dh:].astype(BF16)

    def recur(i):
        for cc in range(GDN_PREP_CHUNKS):
            c = i * GDN_PREP_CHUNKS + cc
            r0 = c * C if isinstance(c, int) else pl.multiple_of(c * C, C)
            for h in range(HB):
                S = state_s[h]
                r = _dot(wq_s[h, c], S.astype(BF16))
                v_new = u_s[h, pl.ds(r0, C), :] - r[:C]
                r2 = _dot(ka_s[h, c], v_new.astype(BF16))
                state_s[h] = S * gl_s[h, c][0:1, :] + r2[:Dh]
                o = r[C:] + r2[Dh:]
                z = z_ref[pl.ds(r0, C), h * Dh:(h + 1) * Dh]
                on = o * lax.rsqrt(jnp.mean(o * o, axis=-1, keepdims=True) + EPS) * og_ref[...]
                o_ref[pl.ds(r0, C), h * Dh:(h + 1) * Dh] = (on * _silu(z)).astype(o_ref.dtype)

    n_groups = NC // GDN_PREP_CHUNKS
    prepare(0)

    def body(i, carry):
        recur(i)
        prepare(i + 1)
        return carry

    lax.fori_loop(0, n_groups - 1, body, 0)
    recur(n_groups - 1)


def _gdn_call(proj, bg, o_gain):
    B, T, _ = proj.shape
    H, Dh, HB, TB, C = GDN_HEADS, GDN_HEAD_DIM, GDN_HB, GDN_TB, CHUNK
    G = H // HB
    Wb = HB * Dh
    NC = TB // C
    col = lambda off: pl.BlockSpec((None, TB, Wb), lambda b, g, t: (b, t, off + g))
    return pl.pallas_call(
        _gdn_kernel,
        grid=(B, G, T // TB),
        in_specs=[col(0), col(G), col(2 * G), col(3 * G),
                  pl.BlockSpec((None, TB, LANES), lambda b, g, t: (b, t, 0)),
                  pl.BlockSpec((1, Dh), lambda b, g, t: (0, 0))],
        out_specs=pl.BlockSpec((None, TB, Wb), lambda b, g, t: (b, t, g)),
        out_shape=jax.ShapeDtypeStruct((B, T, H * Dh), BF16),
        scratch_shapes=[pltpu.VMEM((HB, Dh, Dh), F32),
                        pltpu.VMEM((HB, TB, 1), F32),
                        pltpu.VMEM((HB, TB, 1), F32),
                        pltpu.VMEM((HB, TB, Dh), F32),
                        pltpu.VMEM((HB, NC, 2 * C, Dh), BF16),
                        pltpu.VMEM((HB, NC, Dh + C, C), BF16),
                        pltpu.VMEM((HB, NC, 8, Dh), F32)],
        compiler_params=_params(3),
        name="gated_delta",
    )(proj, proj, proj, proj, bg, o_gain.reshape(1, Dh))


SB_Q_PER_STEP = 4
SB_WINDOW = 3 * Q_BLOCK
LOG2_E = 1.4426950408889634
LOG2_TINY_F32 = -126.0
MASKED_LOG2 = -1e30


def _sb_log2_terms(l2):
    neg, pos = jnp.minimum(l2, 0.0), jnp.maximum(l2, 0.0)
    t2 = jnp.log2(1.0 + jnp.exp2(neg - pos))
    return neg - t2, pos + t2


def _suffix_sums(drop, later):
    r = drop.shape[0]
    s = _dot(jnp.concatenate(_split2(drop), axis=0), later)
    return s[:r] + s[r:]


def _sb_kernel(q_ref, k_ref, v_ref, z_ref, qg_ref, later_ref, o_ref):
    step = pl.program_id(2)
    QB, Dh, WK = Q_BLOCK, SB_HEAD_DIM, SB_WINDOW
    NW = WK // QB
    WA = WK - QB
    R = 2 * QB
    diag = (lax.broadcasted_iota(jnp.int32, (R, WK), 1)
            - (lax.broadcasted_iota(jnp.int32, (R, WK), 0) & (QB - 1)))
    low = lax.broadcasted_iota(jnp.int32, (QB, LANES), 1) < Dh
    first_q = step * SB_Q_PER_STEP

    pre, qms = [], []
    for qq in range(SB_Q_PER_STEP):
        qi = first_q + qq
        q2 = q_ref[qq * QB:(qq + 1) * QB, :]
        sq = q2 * q2
        s_lo = jnp.sum(jnp.where(low, sq, 0.0), axis=-1, keepdims=True)
        s_hi = jnp.sum(jnp.where(low, 0.0, sq), axis=-1, keepdims=True)
        ms = jnp.where(low, s_lo, s_hi) * (1.0 / Dh)
        qn = q2 * lax.rsqrt(ms + EPS) * qg_ref[...] * (LOG2_E * Dh ** -0.5)
        qm2 = jnp.concatenate([jnp.where(low, qn, 0.0), jnp.where(low, 0.0, qn)], axis=0).astype(BF16)
        qms.append(qm2)
        w0 = jnp.maximum(qi - (NW - 1), 0)
        k0 = pl.multiple_of(w0 * QB, QB)
        pre.append((k0, qi - w0, _dot_nt(qm2, k_ref[pl.ds(k0, WK), :])))

    mid = []
    for k0, dblk, l2 in pre:
        mask = diag < dblk * QB
        log2_sig, drop = _sb_log2_terms(l2)
        log2_sig = jnp.where(mask, log2_sig, MASKED_LOG2)
        drop = jnp.where(mask, drop, 0.0)
        drop_a, drop_b = drop[:, :WA], drop[:, WA:]
        mid.append((k0, log2_sig, _suffix_sums(drop_a, later_ref[...]),
                    _suffix_sums(drop_b, later_ref[:QB, :QB]),
                    jnp.sum(drop_a, axis=-1, keepdims=True), jnp.sum(drop_b, axis=-1, keepdims=True)))

    accs, runs = [], []
    for k0, log2_sig, after_a, after_b, tot_a, tot_b in mid:
        after = jnp.concatenate([after_a + tot_b, after_b], axis=1)
        w = jnp.exp2(log2_sig - after).astype(BF16)
        accs.append(_dot(w, v_ref[pl.ds(k0, WK), :]))
        runs.append(tot_a + tot_b)

    def unfinished(run):
        return jnp.min(run) <= -LOG2_TINY_F32

    def earlier_blocks():
        out = []
        for qq in range(SB_Q_PER_STEP):
            def cond(c):
                j, _, run = c
                return jnp.logical_and(j >= 0, unfinished(run))

            def body(c, qm2=qms[qq]):
                j, acc, run = c
                r0 = pl.multiple_of(j * QB, QB)
                log2_sig, drop = _sb_log2_terms(_dot_nt(qm2, k_ref[pl.ds(r0, QB), :]))
                w = jnp.exp2(log2_sig - (_suffix_sums(drop, later_ref[:QB, :QB]) + run)).astype(BF16)
                acc = acc + _dot(w, v_ref[pl.ds(r0, QB), :])
                return j - 1, acc, run + jnp.sum(drop, axis=-1, keepdims=True)

            out.append(lax.while_loop(cond, body, (first_q + qq - NW, accs[qq], runs[qq]))[1])
        return out

    slowest = functools.reduce(jnp.minimum, runs)
    any_left = jnp.logical_and(first_q + SB_Q_PER_STEP - 1 >= NW, unfinished(slowest))
    accs = lax.cond(any_left, earlier_blocks, lambda: accs)
    for qq, acc in enumerate(accs):
        rows = slice(qq * QB, (qq + 1) * QB)
        o_ref[rows, :] = (jnp.where(low, acc[:QB], acc[QB:]) * _silu(z_ref[rows, :])).astype(o_ref.dtype)


def _sb_call(proj, kv, q_gain):
    B, T, W2 = proj.shape
    W = W2 // 2
    nb = W // LANES
    tq = SB_Q_PER_STEP * Q_BLOCK
    wa = SB_WINDOW - Q_BLOCK
    qg = jnp.tile(q_gain, LANES // SB_HEAD_DIM).reshape(1, LANES)
    later = jnp.tri(wa, k=-1, dtype=BF16)
    return pl.pallas_call(
        _sb_kernel,
        grid=(B, nb, T // tq),
        in_specs=[pl.BlockSpec((None, tq, LANES), lambda b, p, i: (b, i, p)),
                  pl.BlockSpec((None, T, LANES), lambda b, p, i: (b, 0, p)),
                  pl.BlockSpec((None, T, LANES), lambda b, p, i: (b, 0, nb + p)),
                  pl.BlockSpec((None, tq, LANES), lambda b, p, i: (b, i, nb + p)),
                  pl.BlockSpec((1, LANES), lambda b, p, i: (0, 0)),
                  pl.BlockSpec((wa, wa), lambda b, p, i: (0, 0))],
        out_specs=pl.BlockSpec((None, tq, LANES), lambda b, p, i: (b, i, p)),
        out_shape=jax.ShapeDtypeStruct((B, T, W), BF16),
        compiler_params=_params(3),
        name="stick_breaking",
    )(proj, kv, kv, proj, qg, later)


def kernel(x, c, norm_g, w_ada, b_ada, w_in_a, conv_w_a, a_log_a, dt_bias_a, o_gain_a, w_out_a,
           kv_norm_g, w_ada_kv, b_ada_kv, w_kv, k_gain, w_in_b, q_gain_b, w_out_b):
    B, T, D = x.shape
    depth = w_ada.shape[0]
    n_a = w_in_a.shape[0]
    gw = GDN_HEADS * GDN_HEAD_DIM

    mod4 = _mod_call(c, w_ada, b_ada).reshape(depth, B, 1, 3 * D)
    modkv4 = _mod_call(c, w_ada_kv[None], b_ada_kv[None]).reshape(1, B, 1, 2 * D)

    kv = None
    for layer in range(depth):
        if layer < n_a:
            i = layer
            w_main = w_in_a[i, :, :4 * gw].astype(BF16)
            w_tail = jnp.pad(w_in_a[i, :, 4 * gw:], ((0, 0), (0, LANES - 2 * GDN_HEADS))).astype(BF16)
            gate_row = lambda vec: jnp.zeros((1, LANES), F32).at[0, GDN_HEADS:2 * GDN_HEADS].set(vec)
            proj, bg = _norm_proj_call(
                _gdn_in_proj_kernel, x, norm_g[layer], mod4, layer, [w_main, w_tail],
                extra=(conv_w_a[i], gate_row(a_log_a[i]), gate_row(dt_bias_a[i])), name="gdn_in_proj",
                scratch_shapes=[pltpu.VMEM((HALO + ROW_TILE, 3 * gw), F32)])
            o = _gdn_call(proj, bg, o_gain_a[i])
            x = _outproj_call(o, w_out_a[i].astype(BF16), x, mod4, layer)
        else:
            if kv is None:
                kg = jnp.tile(k_gain, LANES // SB_HEAD_DIM).reshape(1, LANES)
                (kv,) = _norm_proj_call(_norm_kv_kernel, x, kv_norm_g, modkv4, 0,
                                        [w_kv.astype(BF16)], extra=(kg,), name="shared_kv",
                                        out_dtype=BF16)
            i = layer - n_a
            (proj,) = _norm_proj_call(_norm_proj_kernel, x, norm_g[layer], mod4, layer,
                                      [w_in_b[i].astype(BF16)], name="sb_in_proj")
            o = _sb_call(proj, kv, q_gain_b[i])
            x = _outproj_call(o, w_out_b[i].astype(BF16), x, mod4, layer)
    return x
```

```python
import functools

import jax
import jax.numpy as jnp
from jax import lax
from jax.experimental import pallas as pl
from jax.experimental.pallas import tpu as pltpu

F32 = jnp.float32
BF16 = jnp.bfloat16
HI = lax.Precision.HIGHEST

EPS = 1e-6
CHUNK = 64
GDN_HEADS = 8
GDN_HEAD_DIM = 128
CONV_WIDTH = 4
SB_HEADS = 16
SB_HEAD_DIM = 64
Q_BLOCK = 128
LANES = 128
VMEM_LIMIT = 48 * 1024 * 1024
ROW_TILE = 256
N_CHUNK = 512
HALO = 8


def _params(n_axes):
    return pltpu.CompilerParams(dimension_semantics=("arbitrary",) * n_axes,
                                vmem_limit_bytes=VMEM_LIMIT)


def _silu(x):
    return x * jax.nn.sigmoid(x)


def _dot(a, b, precision=None):
    return jnp.dot(a, b, precision=precision, preferred_element_type=F32)


def _dot_nt(a, b, precision=None):
    return lax.dot_general(a, b, (((1,), (1,)), ((), ())), precision=precision,
                           preferred_element_type=F32)


def _mod_kernel(c_ref, w_ref, b_ref, o_ref):
    ca = _silu(c_ref[...])
    o_ref[...] = _dot(ca, w_ref[...], HI) + b_ref[...]


def _mod_call(c, w, b):
    L, D, N = w.shape
    B = c.shape[0]
    tn = 1024
    return pl.pallas_call(
        _mod_kernel,
        grid=(L, N // tn),
        in_specs=[pl.BlockSpec((B, D), lambda l, j: (0, 0)),
                  pl.BlockSpec((None, D, tn), lambda l, j: (l, 0, j)),
                  pl.BlockSpec((None, 1, tn), lambda l, j: (l, 0, j))],
        out_specs=pl.BlockSpec((None, B, tn), lambda l, j: (l, 0, j)),
        out_shape=jax.ShapeDtypeStruct((L, B, N), F32),
        compiler_params=_params(2),
        name="adaln_mod",
    )(c, w, b.reshape(L, 1, N))


def _modnorm(x_ref, g_ref, sh_ref, sc_ref):
    x = x_ref[...]
    n = x * lax.rsqrt(jnp.mean(x * x, axis=-1, keepdims=True) + EPS) * g_ref[...]
    return (n * (1.0 + sc_ref[...]) + sh_ref[...]).astype(BF16)


def _proj_cols(hb, w_ref, o_ref):
    n = w_ref.shape[1]
    for j in range(0, n, N_CHUNK):
        o_ref[:, j:j + N_CHUNK] = _dot(hb, w_ref[:, j:j + N_CHUNK])


def _norm_proj_kernel(x_ref, g_ref, sh_ref, sc_ref, w_ref, o_ref):
    _proj_cols(_modnorm(x_ref, g_ref, sh_ref, sc_ref), w_ref, o_ref)


def _gdn_in_proj_kernel(x_ref, g_ref, sh_ref, sc_ref, w_ref, wt_ref, cw_ref, alog_ref, dtb_ref,
                        o_ref, bg_ref, xe_s):
    tm = x_ref.shape[0]
    Dh = GDN_HEAD_DIM
    w3 = cw_ref.shape[1]
    wd = w3 // 3

    @pl.when(pl.program_id(1) == 0)
    def _():
        xe_s[:HALO, :] = jnp.zeros((HALO, w3), F32)

    hb = _modnorm(x_ref, g_ref, sh_ref, sc_ref)
    for j in range(0, w3, N_CHUNK):
        cols = slice(j, j + N_CHUNK)
        raw = _dot(hb, w_ref[:, cols])
        xe_s[HALO:, cols] = raw
        cw = cw_ref[:, cols]
        acc = raw * cw[CONV_WIDTH - 1:CONV_WIDTH, :]
        for s in range(1, CONV_WIDTH):
            acc = acc + xe_s[HALO - s:HALO - s + tm, cols] * cw[CONV_WIDTH - 1 - s:CONV_WIDTH - s, :]
        xe_s[:HALO, cols] = raw[tm - HALO:]
        y = _silu(acc)
        if j < 2 * wd:
            scale = Dh ** -0.5 if j < wd else 1.0
            for h0 in range(0, N_CHUNK, Dh):
                yh = y[:, h0:h0 + Dh]
                inv = lax.rsqrt(jnp.sum(yh * yh, axis=-1, keepdims=True) + EPS) * scale
                o_ref[:, j + h0:j + h0 + Dh] = yh * inv
        else:
            o_ref[:, cols] = y
    for j in range(w3, w_ref.shape[1], N_CHUNK):
        o_ref[:, j:j + N_CHUNK] = _dot(hb, w_ref[:, j:j + N_CHUNK])
    t = _dot(hb, wt_ref[...])
    a = t + dtb_ref[...]
    g = -jnp.exp(alog_ref[...]) * (jnp.maximum(a, 0.0) + jnp.log1p(jnp.exp(-jnp.abs(a))))
    lane = lax.broadcasted_iota(jnp.int32, t.shape, 1)
    bg_ref[...] = jnp.where(lane < GDN_HEADS, jax.nn.sigmoid(t), g)


def _norm_kv_kernel(x_ref, g_ref, sh_ref, sc_ref, w_ref, kg_ref, o_ref):
    hb = _modnorm(x_ref, g_ref, sh_ref, sc_ref)
    n = w_ref.shape[1]
    half = n // 2
    tm = x_ref.shape[0]
    low = lax.broadcasted_iota(jnp.int32, (tm, LANES), 1) < SB_HEAD_DIM
    for j in range(0, half, LANES):
        kk = _dot(hb, w_ref[:, j:j + LANES])
        sq = kk * kk
        s_lo = jnp.sum(jnp.where(low, sq, 0.0), axis=-1, keepdims=True)
        s_hi = jnp.sum(jnp.where(low, 0.0, sq), axis=-1, keepdims=True)
        ms = jnp.where(low, s_lo, s_hi) * (1.0 / SB_HEAD_DIM)
        o_ref[:, j:j + LANES] = (kk * lax.rsqrt(ms + EPS) * kg_ref[...]).astype(o_ref.dtype)
    for j in range(half, n, N_CHUNK):
        o_ref[:, j:j + N_CHUNK] = _dot(hb, w_ref[:, j:j + N_CHUNK]).astype(o_ref.dtype)


def _norm_proj_call(kernel_fn, x, gain, mod4, layer, w_list, extra=(), name="norm_proj",
                    out_dtype=F32, scratch_shapes=()):
    B, T, D = x.shape
    tm = ROW_TILE
    in_specs = [pl.BlockSpec((None, tm, D), lambda b, i: (b, i, 0)),
                pl.BlockSpec((1, D), lambda b, i: (0, 0)),
                pl.BlockSpec((None, None, 1, D), lambda b, i: (layer, b, 0, 0)),
                pl.BlockSpec((None, None, 1, D), lambda b, i: (layer, b, 0, 1))]
    out_specs, out_shape = [], []
    for w in w_list:
        n = w.shape[1]
        in_specs.append(pl.BlockSpec((D, n), lambda b, i: (0, 0)))
        out_specs.append(pl.BlockSpec((None, tm, n), lambda b, i: (b, i, 0)))
        out_shape.append(jax.ShapeDtypeStruct((B, T, n), out_dtype))
    for e in extra:
        in_specs.append(pl.BlockSpec(e.shape, lambda b, i: (0, 0)))
    return pl.pallas_call(
        kernel_fn,
        grid=(B, T // tm),
        in_specs=in_specs,
        out_specs=out_specs,
        out_shape=out_shape,
        scratch_shapes=list(scratch_shapes),
        compiler_params=_params(2),
        name=name,
    )(x, gain.reshape(1, D), mod4, mod4, *w_list, *extra)


def _outproj_kernel(o_ref, w_ref, x_ref, gate_ref, xo_ref):
    y = _dot(o_ref[...], w_ref[...])
    xo_ref[...] = x_ref[...] + gate_ref[...] * y


def _outproj_call(o, w, x, mod4, layer):
    B, T, D = x.shape
    W = o.shape[-1]
    tm = ROW_TILE
    return pl.pallas_call(
        _outproj_kernel,
        grid=(B, T // tm),
        in_specs=[pl.BlockSpec((None, tm, W), lambda b, i: (b, i, 0)),
                  pl.BlockSpec((W, D), lambda b, i: (0, 0)),
                  pl.BlockSpec((None, tm, D), lambda b, i: (b, i, 0)),
                  pl.BlockSpec((None, None, 1, D), lambda b, i: (layer, b, 0, 2))],
        out_specs=pl.BlockSpec((None, tm, D), lambda b, i: (b, i, 0)),
        out_shape=jax.ShapeDtypeStruct((B, T, D), F32),
        compiler_params=_params(2),
        name="outproj_residual",
    )(o, w, x, mod4)


GDN_HB = 8
GDN_TB = 512
GDN_PREP_CHUNKS = 2


def _split2(x):
    hi = x.astype(BF16)
    return hi, (x - hi.astype(F32)).astype(BF16)


def _dot_split_pairs(lhs_rows, p):
    n = p.shape[0]
    m = len(lhs_rows)
    left = lax.broadcasted_iota(jnp.int32, p.shape, 1) < n
    block_diag = jnp.concatenate([jnp.where(left, p, 0.0), jnp.where(left, 0.0, p)], axis=0)
    his, los = zip(*[_split2(a) for a in lhs_rows])
    ph, plo = _split2(block_diag)
    r1 = _dot(jnp.concatenate(his + los, axis=0), ph)
    r2 = _dot(jnp.concatenate(his, axis=0), plo)
    return [r1[i * n:(i + 1) * n] + r1[(m + i) * n:(m + i + 1) * n] + r2[i * n:(i + 1) * n]
            for i in range(m)]


def _unit_lower_inverses(Ls):
    n = Ls[0].shape[0]
    pairs = [jnp.concatenate([a, b], axis=1) for a, b in zip(Ls[0::2], Ls[1::2])]
    ri = lax.broadcasted_iota(jnp.int32, (n, 2 * n), 0)
    ci = lax.broadcasted_iota(jnp.int32, (n, 2 * n), 1) & (n - 1)
    eye = jnp.where(ri == ci, 1.0, 0.0)
    Xs = [eye - L for L in pairs]
    Ps = [_dot_split_pairs([L], L)[0] for L in pairs]
    k = 4
    while k < n:
        outs = [_dot_split_pairs([X, P], P) for X, P in zip(Xs, Ps)]
        Xs = [X + o[0] for X, o in zip(Xs, outs)]
        Ps = [o[1] for o in outs]
        k *= 2
    out = []
    for X, P in zip(Xs, Ps):
        T = X + _dot_split_pairs([X], P)[0]
        out += [T[:, :n], T[:, n:]]
    return out


def _gdn_kernel(q_ref, k_ref, v_ref, z_ref, bg_ref, og_ref, o_ref,
                state_s, beta_s, g_s, u_s, wq_s, ka_s, gl_s):
    hg = pl.program_id(1)
    tb = pl.program_id(2)
    TB = q_ref.shape[0]
    HB, Dh, C = GDN_HB, GDN_HEAD_DIM, CHUNK
    NC = TB // C

    @pl.when(tb == 0)
    def _():
        state_s[...] = jnp.zeros_like(state_s)

    lane = lax.broadcasted_iota(jnp.int32, (TB, LANES), 1)
    bg = bg_ref[...]
    for h in range(HB):
        head = hg * HB + h
        beta_s[h] = jnp.sum(jnp.where(lane == head, bg, 0.0), axis=-1, keepdims=True)
        g_s[h] = jnp.sum(jnp.where(lane == head + GDN_HEADS, bg, 0.0), axis=-1, keepdims=True)

    ri = lax.broadcasted_iota(jnp.int32, (C, C), 0)
    ci = lax.broadcasted_iota(jnp.int32, (C, C), 1)
    tril = ri >= ci
    strict = ri > ci
    eye = ri == ci

    def to_row(col):
        return jnp.sum(jnp.where(eye, jnp.broadcast_to(col, (C, C)), 0.0), axis=0, keepdims=True)

    def prepare(i):
        chains = [(h, i * GDN_PREP_CHUNKS + cc) for cc in range(GDN_PREP_CHUNKS) for h in range(HB)]
        staged = []
        for h, c in chains:
            r0 = c * C if isinstance(c, int) else pl.multiple_of(c * C, C)
            q = q_ref[pl.ds(r0, C), h * Dh:(h + 1) * Dh]
            k = k_ref[pl.ds(r0, C), h * Dh:(h + 1) * Dh]
            v = v_ref[pl.ds(r0, C), h * Dh:(h + 1) * Dh]
            beta = beta_s[h, pl.ds(r0, C), :]
            g_row = to_row(g_s[h, pl.ds(r0, C), :])
            gc_col = jnp.sum(jnp.where(tril, jnp.broadcast_to(g_row, (C, C)), 0.0),
                             axis=1, keepdims=True)
            gc_row = to_row(gc_col)
            g_last = gc_col[C - 1:C, :]
            decay = jnp.where(tril, jnp.exp(jnp.where(tril, gc_col - gc_row, 0.0)), 0.0)
            kb = k * beta
            egc = jnp.exp(gc_col)
            kq = _dot_nt(jnp.concatenate([kb, q], axis=0).astype(BF16), k.astype(BF16))
            L = jnp.where(strict, kq[:C] * decay, 0.0)
            attn = jnp.where(tril, kq[C:] * decay, 0.0)
            rhs = jnp.concatenate([v * beta, kb * egc], axis=1).astype(BF16)
            k_dec = k * jnp.exp(g_last - gc_col)
            wq_s[h, c, C:, :] = (q * egc).astype(BF16)
            ka_s[h, c, :Dh, :] = k_dec.T.astype(BF16)
            ka_s[h, c, Dh:, :] = attn.astype(BF16)
            gl_s[h, c] = jnp.broadcast_to(jnp.exp(g_last), (8, Dh))
            staged.append((h, c, r0, L, rhs))
        tinvs = _unit_lower_inverses([s[3] for s in staged])
        for (h, c, r0, _, rhs), tinv in zip(staged, tinvs):
            uw = _dot(tinv.astype(BF16), rhs)
            u_s[h, pl.ds(r0, C), :] = uw[:, :Dh]
            wq_s[h, c, :C, :] = uw[:, Dh:].astype(BF16)

    def recur(i):
        for cc in range(GDN_PREP_CHUNKS):
            c = i * GDN_PREP_CHUNKS + cc
            r0 = c * C if isinstance(c, int) else pl.multiple_of(c * C, C)
            for h in range(HB):
                S = state_s[h]
                r = _dot(wq_s[h, c], S.astype(BF16))
                v_new = u_s[h, pl.ds(r0, C), :] - r[:C]
                r2 = _dot(ka_s[h, c], v_new.astype(BF16))
                state_s[h] = S * gl_s[h, c][0:1, :] + r2[:Dh]
                o = r[C:] + r2[Dh:]
                z = z_ref[pl.ds(r0, C), h * Dh:(h + 1) * Dh]
                on = o * lax.rsqrt(jnp.mean(o * o, axis=-1, keepdims=True) + EPS) * og_ref[...]
                o_ref[pl.ds(r0, C), h * Dh:(h + 1) * Dh] = (on * _silu(z)).astype(o_ref.dtype)

    n_groups = NC // GDN_PREP_CHUNKS
    prepare(0)

    def body(i, carry):
        recur(i)
        prepare(i + 1)
        return carry

    lax.fori_loop(0, n_groups - 1, body, 0)
    recur(n_groups - 1)


def _gdn_call(proj, bg, o_gain):
    B, T, _ = proj.shape
    H, Dh, HB, TB, C = GDN_HEADS, GDN_HEAD_DIM, GDN_HB, GDN_TB, CHUNK
    G = H // HB
    Wb = HB * Dh
    NC = TB // C
    col = lambda off: pl.BlockSpec((None, TB, Wb), lambda b, g, t: (b, t, off + g))
    return pl.pallas_call(
        _gdn_kernel,
        grid=(B, G, T // TB),
        in_specs=[col(0), col(G), col(2 * G), col(3 * G),
                  pl.BlockSpec((None, TB, LANES), lambda b, g, t: (b, t, 0)),
                  pl.BlockSpec((1, Dh), lambda b, g, t: (0, 0))],
        out_specs=pl.BlockSpec((None, TB, Wb), lambda b, g, t: (b, t, g)),
        out_shape=jax.ShapeDtypeStruct((B, T, H * Dh), BF16),
        scratch_shapes=[pltpu.VMEM((HB, Dh, Dh), F32),
                        pltpu.VMEM((HB, TB, 1), F32),
                        pltpu.VMEM((HB, TB, 1), F32),
                        pltpu.VMEM((HB, TB, Dh), F32),
                        pltpu.VMEM((HB, NC, 2 * C, Dh), BF16),
                        pltpu.VMEM((HB, NC, Dh + C, C), BF16),
                        pltpu.VMEM((HB, NC, 8, Dh), F32)],
        compiler_params=_params(3),
        name="gated_delta",
    )(proj, proj, proj, proj, bg, o_gain.reshape(1, Dh))


SB_Q_PER_STEP = 8
SB_WINDOW = 3 * Q_BLOCK
LOG2_E = 1.4426950408889634
LOG2_TINY_F32 = -126.0
MASKED_LOG2 = -1e30


def _sb_log2_terms(l2):
    neg, pos = jnp.minimum(l2, 0.0), jnp.maximum(l2, 0.0)
    t2 = jnp.log2(1.0 + jnp.exp2(neg - pos))
    return neg - t2, pos + t2


def _suffix_sums(drop, later):
    r = drop.shape[0]
    s = _dot(jnp.concatenate(_split2(drop), axis=0), later)
    return s[:r] + s[r:]


def _sb_kernel(q_ref, k_ref, v_ref, z_ref, qg_ref, later_ref, o_ref):
    step = pl.program_id(2)
    QB, Dh, WK = Q_BLOCK, SB_HEAD_DIM, SB_WINDOW
    NW = WK // QB
    WA = WK - QB
    R = 2 * QB
    diag = (lax.broadcasted_iota(jnp.int32, (R, WK), 1)
            - (lax.broadcasted_iota(jnp.int32, (R, WK), 0) & (QB - 1)))
    low = lax.broadcasted_iota(jnp.int32, (QB, LANES), 1) < Dh
    first_q = step * SB_Q_PER_STEP

    pre, qms = [], []
    for qq in range(SB_Q_PER_STEP):
        qi = first_q + qq
        q2 = q_ref[qq * QB:(qq + 1) * QB, :]
        sq = q2 * q2
        s_lo = jnp.sum(jnp.where(low, sq, 0.0), axis=-1, keepdims=True)
        s_hi = jnp.sum(jnp.where(low, 0.0, sq), axis=-1, keepdims=True)
        ms = jnp.where(low, s_lo, s_hi) * (1.0 / Dh)
        qn = q2 * lax.rsqrt(ms + EPS) * qg_ref[...] * (LOG2_E * Dh ** -0.5)
        qm2 = jnp.concatenate([jnp.where(low, qn, 0.0), jnp.where(low, 0.0, qn)], axis=0).astype(BF16)
        qms.append(qm2)
        w0 = jnp.maximum(qi - (NW - 1), 0)
        k0 = pl.multiple_of(w0 * QB, QB)
        pre.append((k0, qi - w0, _dot_nt(qm2, k_ref[pl.ds(k0, WK), :])))

    mid = []
    for k0, dblk, l2 in pre:
        mask = diag < dblk * QB
        log2_sig, drop = _sb_log2_terms(l2)
        log2_sig = jnp.where(mask, log2_sig, MASKED_LOG2)
        drop = jnp.where(mask, drop, 0.0)
        drop_a, drop_b = drop[:, :WA], drop[:, WA:]
        mid.append((k0, log2_sig, _suffix_sums(drop_a, later_ref[...]),
                    _suffix_sums(drop_b, later_ref[:QB, :QB]),
                    jnp.sum(drop_a, axis=-1, keepdims=True), jnp.sum(drop_b, axis=-1, keepdims=True)))

    accs, runs = [], []
    for k0, log2_sig, after_a, after_b, tot_a, tot_b in mid:
        after = jnp.concatenate([after_a + tot_b, after_b], axis=1)
        w = jnp.exp2(log2_sig - after).astype(BF16)
        accs.append(_dot(w, v_ref[pl.ds(k0, WK), :]))
        runs.append(tot_a + tot_b)

    def unfinished(run):
        return jnp.min(run) <= -LOG2_TINY_F32

    def earlier_blocks():
        out = []
        for qq in range(SB_Q_PER_STEP):
            def cond(c):
                j, _, run = c
                return jnp.logical_and(j >= 0, unfinished(run))

            def body(c, qm2=qms[qq]):
                j, acc, run = c
                r0 = pl.multiple_of(j * QB, QB)
                log2_sig, drop = _sb_log2_terms(_dot_nt(qm2, k_ref[pl.ds(r0, QB), :]))
                w = jnp.exp2(log2_sig - (_suffix_sums(drop, later_ref[:QB, :QB]) + run)).astype(BF16)
                acc = acc + _dot(w, v_ref[pl.ds(r0, QB), :])
                return j - 1, acc, run + jnp.sum(drop, axis=-1, keepdims=True)

            out.append(lax.while_loop(cond, body, (first_q + qq - NW, accs[qq], runs[qq]))[1])
        return out

    slowest = functools.reduce(jnp.minimum, runs)
    any_left = jnp.logical_and(first_q + SB_Q_PER_STEP - 1 >= NW, unfinished(slowest))
    accs = lax.cond(any_left, earlier_blocks, lambda: accs)
    for qq, acc in enumerate(accs):
        rows = slice(qq * QB, (qq + 1) * QB)
        o_ref[rows, :] = (jnp.where(low, acc[:QB], acc[QB:]) * _silu(z_ref[rows, :])).astype(o_ref.dtype)


def _sb_call(proj, kv, q_gain):
    B, T, W2 = proj.shape
    W = W2 // 2
    nb = W // LANES
    tq = SB_Q_PER_STEP * Q_BLOCK
    wa = SB_WINDOW - Q_BLOCK
    qg = jnp.tile(q_gain, LANES // SB_HEAD_DIM).reshape(1, LANES)
    later = jnp.tri(wa, k=-1, dtype=BF16)
    return pl.pallas_call(
        _sb_kernel,
        grid=(B, nb, T // tq),
        in_specs=[pl.BlockSpec((None, tq, LANES), lambda b, p, i: (b, i, p)),
                  pl.BlockSpec((None, T, LANES), lambda b, p, i: (b, 0, p)),
                  pl.BlockSpec((None, T, LANES), lambda b, p, i: (b, 0, nb + p)),
                  pl.BlockSpec((None, tq, LANES), lambda b, p, i: (b, i, nb + p)),
                  pl.BlockSpec((1, LANES), lambda b, p, i: (0, 0)),
                  pl.BlockSpec((wa, wa), lambda b, p, i: (0, 0))],
        out_specs=pl.BlockSpec((None, tq, LANES), lambda b, p, i: (b, i, p)),
        out_shape=jax.ShapeDtypeStruct((B, T, W), BF16),
        compiler_params=_params(3),
        name="stick_breaking",
    )(proj, kv, kv, proj, qg, later)


def kernel(x, c, norm_g, w_ada, b_ada, w_in_a, conv_w_a, a_log_a, dt_bias_a, o_gain_a, w_out_a,
           kv_norm_g, w_ada_kv, b_ada_kv, w_kv, k_gain, w_in_b, q_gain_b, w_out_b):
    B, T, D = x.shape
    depth = w_ada.shape[0]
    n_a = w_in_a.shape[0]
    gw = GDN_HEADS * GDN_HEAD_DIM

    mod4 = _mod_call(c, w_ada, b_ada).reshape(depth, B, 1, 3 * D)
    modkv4 = _mod_call(c, w_ada_kv[None], b_ada_kv[None]).reshape(1, B, 1, 2 * D)

    kv = None
    for layer in range(depth):
        if layer < n_a:
            i = layer
            w_main = w_in_a[i, :, :4 * gw].astype(BF16)
            w_tail = jnp.pad(w_in_a[i, :, 4 * gw:], ((0, 0), (0, LANES - 2 * GDN_HEADS))).astype(BF16)
            gate_row = lambda vec: jnp.zeros((1, LANES), F32).at[0, GDN_HEADS:2 * GDN_HEADS].set(vec)
            proj, bg = _norm_proj_call(
                _gdn_in_proj_kernel, x, norm_g[layer], mod4, layer, [w_main, w_tail],
                extra=(conv_w_a[i], gate_row(a_log_a[i]), gate_row(dt_bias_a[i])), name="gdn_in_proj",
                scratch_shapes=[pltpu.VMEM((HALO + ROW_TILE, 3 * gw), F32)])
            o = _gdn_call(proj, bg, o_gain_a[i])
            x = _outproj_call(o, w_out_a[i].astype(BF16), x, mod4, layer)
        else:
            if kv is None:
                kg = jnp.tile(k_gain, LANES // SB_HEAD_DIM).reshape(1, LANES)
                (kv,) = _norm_proj_call(_norm_kv_kernel, x, kv_norm_g, modkv4, 0,
                                        [w_kv.astype(BF16)], extra=(kg,), name="shared_kv",
                                        out_dtype=BF16)
            i = layer - n_a
            (proj,) = _norm_proj_call(_norm_proj_kernel, x, norm_g[layer], mod4, layer,
                                      [w_in_b[i].astype(BF16)], name="sb_in_proj")
            o = _sb_call(proj, kv, q_gain_b[i])
            x = _outproj_call(o, w_out_b[i].astype(BF16), x, mod4, layer)
    return x
```

```python
import functools

import jax
import jax.numpy as jnp
from jax import lax
from jax.experimental import pallas as pl
from jax.experimental.pallas import tpu as pltpu

F32 = jnp.float32
BF16 = jnp.bfloat16
HI = lax.Precision.HIGHEST

EPS = 1e-6
CHUNK = 64
GDN_HEADS = 8
GDN_HEAD_DIM = 128
CONV_WIDTH = 4
SB_HEADS = 16
SB_HEAD_DIM = 64
Q_BLOCK = 128
LANES = 128
VMEM_LIMIT = 48 * 1024 * 1024
ROW_TILE = 256
WIDE_ROW_TILE = 512
N_CHUNK = 512
HALO = 8


def _params(n_axes):
    return pltpu.CompilerParams(dimension_semantics=("arbitrary",) * n_axes,
                                vmem_limit_bytes=VMEM_LIMIT)


def _silu(x):
    return x * jax.nn.sigmoid(x)


def _dot(a, b, precision=None):
    return jnp.dot(a, b, precision=precision, preferred_element_type=F32)


def _dot_nt(a, b, precision=None):
    return lax.dot_general(a, b, (((1,), (1,)), ((), ())), precision=precision,
                           preferred_element_type=F32)


def _mod_kernel(c_ref, w_ref, b_ref, o_ref):
    ca = _silu(c_ref[...])
    o_ref[...] = _dot(ca, w_ref[...], HI) + b_ref[...]


def _mod_call(c, w, b):
    L, D, N = w.shape
    B = c.shape[0]
    tn = 1024
    return pl.pallas_call(
        _mod_kernel,
        grid=(L, N // tn),
        in_specs=[pl.BlockSpec((B, D), lambda l, j: (0, 0)),
                  pl.BlockSpec((None, D, tn), lambda l, j: (l, 0, j)),
                  pl.BlockSpec((None, 1, tn), lambda l, j: (l, 0, j))],
        out_specs=pl.BlockSpec((None, B, tn), lambda l, j: (l, 0, j)),
        out_shape=jax.ShapeDtypeStruct((L, B, N), F32),
        compiler_params=_params(2),
        name="adaln_mod",
    )(c, w, b.reshape(L, 1, N))


def _modnorm(x_ref, g_ref, sh_ref, sc_ref):
    x = x_ref[...]
    n = x * lax.rsqrt(jnp.mean(x * x, axis=-1, keepdims=True) + EPS) * g_ref[...]
    return (n * (1.0 + sc_ref[...]) + sh_ref[...]).astype(BF16)


def _proj_cols(hb, w_ref, o_ref):
    n = w_ref.shape[1]
    for j in range(0, n, N_CHUNK):
        o_ref[:, j:j + N_CHUNK] = _dot(hb, w_ref[:, j:j + N_CHUNK])


def _norm_proj_kernel(x_ref, g_ref, sh_ref, sc_ref, w_ref, o_ref):
    _proj_cols(_modnorm(x_ref, g_ref, sh_ref, sc_ref), w_ref, o_ref)


def _gdn_in_proj_kernel(x_ref, g_ref, sh_ref, sc_ref, w_ref, wt_ref, cw_ref, alog_ref, dtb_ref,
                        o_ref, bg_ref, xe_s):
    tm = x_ref.shape[0]
    Dh = GDN_HEAD_DIM
    w3 = cw_ref.shape[1]
    wd = w3 // 3

    @pl.when(pl.program_id(1) == 0)
    def _():
        xe_s[:HALO, :] = jnp.zeros((HALO, w3), F32)

    hb = _modnorm(x_ref, g_ref, sh_ref, sc_ref)
    for j in range(0, w3, N_CHUNK):
        cols = slice(j, j + N_CHUNK)
        raw = _dot(hb, w_ref[:, cols])
        xe_s[HALO:, cols] = raw
        cw = cw_ref[:, cols]
        acc = raw * cw[CONV_WIDTH - 1:CONV_WIDTH, :]
        for s in range(1, CONV_WIDTH):
            acc = acc + xe_s[HALO - s:HALO - s + tm, cols] * cw[CONV_WIDTH - 1 - s:CONV_WIDTH - s, :]
        xe_s[:HALO, cols] = raw[tm - HALO:]
        y = _silu(acc)
        if j < 2 * wd:
            scale = Dh ** -0.5 if j < wd else 1.0
            for h0 in range(0, N_CHUNK, Dh):
                yh = y[:, h0:h0 + Dh]
                inv = lax.rsqrt(jnp.sum(yh * yh, axis=-1, keepdims=True) + EPS) * scale
                o_ref[:, j + h0:j + h0 + Dh] = yh * inv
        else:
            o_ref[:, cols] = y
    for j in range(w3, w_ref.shape[1], N_CHUNK):
        o_ref[:, j:j + N_CHUNK] = _dot(hb, w_ref[:, j:j + N_CHUNK])
    t = _dot(hb, wt_ref[...])
    a = t + dtb_ref[...]
    g = -jnp.exp(alog_ref[...]) * (jnp.maximum(a, 0.0) + jnp.log1p(jnp.exp(-jnp.abs(a))))
    lane = lax.broadcasted_iota(jnp.int32, t.shape, 1)
    bg_ref[...] = jnp.where(lane < GDN_HEADS, jax.nn.sigmoid(t), g)


def _norm_kv_kernel(x_ref, g_ref, sh_ref, sc_ref, w_ref, kg_ref, o_ref):
    hb = _modnorm(x_ref, g_ref, sh_ref, sc_ref)
    n = w_ref.shape[1]
    half = n // 2
    tm = x_ref.shape[0]
    low = lax.broadcasted_iota(jnp.int32, (tm, LANES), 1) < SB_HEAD_DIM
    for j in range(0, half, LANES):
        kk = _dot(hb, w_ref[:, j:j + LANES])
        sq = kk * kk
        s_lo = jnp.sum(jnp.where(low, sq, 0.0), axis=-1, keepdims=True)
        s_hi = jnp.sum(jnp.where(low, 0.0, sq), axis=-1, keepdims=True)
        ms = jnp.where(low, s_lo, s_hi) * (1.0 / SB_HEAD_DIM)
        o_ref[:, j:j + LANES] = (kk * lax.rsqrt(ms + EPS) * kg_ref[...]).astype(o_ref.dtype)
    for j in range(half, n, N_CHUNK):
        o_ref[:, j:j + N_CHUNK] = _dot(hb, w_ref[:, j:j + N_CHUNK]).astype(o_ref.dtype)


def _norm_proj_call(kernel_fn, x, gain, mod4, layer, w_list, extra=(), name="norm_proj",
                    out_dtype=F32, scratch_shapes=(), tm=ROW_TILE):
    B, T, D = x.shape
    in_specs = [pl.BlockSpec((None, tm, D), lambda b, i: (b, i, 0)),
                pl.BlockSpec((1, D), lambda b, i: (0, 0)),
                pl.BlockSpec((None, None, 1, D), lambda b, i: (layer, b, 0, 0)),
                pl.BlockSpec((None, None, 1, D), lambda b, i: (layer, b, 0, 1))]
    out_specs, out_shape = [], []
    for w in w_list:
        n = w.shape[1]
        in_specs.append(pl.BlockSpec((D, n), lambda b, i: (0, 0)))
        out_specs.append(pl.BlockSpec((None, tm, n), lambda b, i: (b, i, 0)))
        out_shape.append(jax.ShapeDtypeStruct((B, T, n), out_dtype))
    for e in extra:
        in_specs.append(pl.BlockSpec(e.shape, lambda b, i: (0, 0)))
    return pl.pallas_call(
        kernel_fn,
        grid=(B, T // tm),
        in_specs=in_specs,
        out_specs=out_specs,
        out_shape=out_shape,
        scratch_shapes=list(scratch_shapes),
        compiler_params=_params(2),
        name=name,
    )(x, gain.reshape(1, D), mod4, mod4, *w_list, *extra)


def _outproj_kernel(o_ref, w_ref, x_ref, gate_ref, xo_ref):
    y = _dot(o_ref[...], w_ref[...])
    xo_ref[...] = x_ref[...] + gate_ref[...] * y


def _outproj_call(o, w, x, mod4, layer):
    B, T, D = x.shape
    W = o.shape[-1]
    tm = WIDE_ROW_TILE
    return pl.pallas_call(
        _outproj_kernel,
        grid=(B, T // tm),
        in_specs=[pl.BlockSpec((None, tm, W), lambda b, i: (b, i, 0)),
                  pl.BlockSpec((W, D), lambda b, i: (0, 0)),
                  pl.BlockSpec((None, tm, D), lambda b, i: (b, i, 0)),
                  pl.BlockSpec((None, None, 1, D), lambda b, i: (layer, b, 0, 2))],
        out_specs=pl.BlockSpec((None, tm, D), lambda b, i: (b, i, 0)),
        out_shape=jax.ShapeDtypeStruct((B, T, D), F32),
        compiler_params=_params(2),
        name="outproj_residual",
    )(o, w, x, mod4)


GDN_HB = 8
GDN_TB = 512
GDN_PREP_CHUNKS = 2


def _split2(x):
    hi = x.astype(BF16)
    return hi, (x - hi.astype(F32)).astype(BF16)


def _dot_split_pairs(lhs_rows, p):
    n = p.shape[0]
    m = len(lhs_rows)
    left = lax.broadcasted_iota(jnp.int32, p.shape, 1) < n
    block_diag = jnp.concatenate([jnp.where(left, p, 0.0), jnp.where(left, 0.0, p)], axis=0)
    his, los = zip(*[_split2(a) for a in lhs_rows])
    ph, plo = _split2(block_diag)
    r1 = _dot(jnp.concatenate(his + los, axis=0), ph)
    r2 = _dot(jnp.concatenate(his, axis=0), plo)
    return [r1[i * n:(i + 1) * n] + r1[(m + i) * n:(m + i + 1) * n] + r2[i * n:(i + 1) * n]
            for i in range(m)]


def _unit_lower_inverses(Ls):
    n = Ls[0].shape[0]
    pairs = [jnp.concatenate([a, b], axis=1) for a, b in zip(Ls[0::2], Ls[1::2])]
    ri = lax.broadcasted_iota(jnp.int32, (n, 2 * n), 0)
    ci = lax.broadcasted_iota(jnp.int32, (n, 2 * n), 1) & (n - 1)
    eye = jnp.where(ri == ci, 1.0, 0.0)
    Xs = [eye - L for L in pairs]
    Ps = [_dot_split_pairs([L], L)[0] for L in pairs]
    k = 4
    while k < n:
        outs = [_dot_split_pairs([X, P], P) for X, P in zip(Xs, Ps)]
        Xs = [X + o[0] for X, o in zip(Xs, outs)]
        Ps = [o[1] for o in outs]
        k *= 2
    out = []
    for X, P in zip(Xs, Ps):
        T = X + _dot_split_pairs([X], P)[0]
        out += [T[:, :n], T[:, n:]]
    return out


def _gdn_kernel(q_ref, k_ref, v_ref, z_ref, bg_ref, og_ref, o_ref,
                state_s, beta_s, g_s, u_s, wq_s, ka_s, gl_s):
    hg = pl.program_id(1)
    tb = pl.program_id(2)
    TB = q_ref.shape[0]
    HB, Dh, C = GDN_HB, GDN_HEAD_DIM, CHUNK
    NC = TB // C

    @pl.when(tb == 0)
    def _():
        state_s[...] = jnp.zeros_like(state_s)

    lane = lax.broadcasted_iota(jnp.int32, (TB, LANES), 1)
    bg = bg_ref[...]
    for h in range(HB):
        head = hg * HB + h
        beta_s[h] = jnp.sum(jnp.where(lane == head, bg, 0.0), axis=-1, keepdims=True)
        g_s[h] = jnp.sum(jnp.where(lane == head + GDN_HEADS, bg, 0.0), axis=-1, keepdims=True)

    ri = lax.broadcasted_iota(jnp.int32, (C, C), 0)
    ci = lax.broadcasted_iota(jnp.int32, (C, C), 1)
    tril = ri >= ci
    strict = ri > ci
    eye = ri == ci

    def to_row(col):
        return jnp.sum(jnp.where(eye, jnp.broadcast_to(col, (C, C)), 0.0), axis=0, keepdims=True)

    def prepare(i):
        chains = [(h, i * GDN_PREP_CHUNKS + cc) for cc in range(GDN_PREP_CHUNKS) for h in range(HB)]
        staged = []
        for h, c in chains:
            r0 = c * C if isinstance(c, int) else pl.multiple_of(c * C, C)
            q = q_ref[pl.ds(r0, C), h * Dh:(h + 1) * Dh]
            k = k_ref[pl.ds(r0, C), h * Dh:(h + 1) * Dh]
            v = v_ref[pl.ds(r0, C), h * Dh:(h + 1) * Dh]
            beta = beta_s[h, pl.ds(r0, C), :]
            g_row = to_row(g_s[h, pl.ds(r0, C), :])
            gc_col = jnp.sum(jnp.where(tril, jnp.broadcast_to(g_row, (C, C)), 0.0),
                             axis=1, keepdims=True)
            gc_row = to_row(gc_col)
            g_last = gc_col[C - 1:C, :]
            decay = jnp.where(tril, jnp.exp(jnp.where(tril, gc_col - gc_row, 0.0)), 0.0)
            kb = k * beta
            egc = jnp.exp(gc_col)
            kq = _dot_nt(jnp.concatenate([kb, q], axis=0).astype(BF16), k.astype(BF16))
            L = jnp.where(strict, kq[:C] * decay, 0.0)
            attn = jnp.where(tril, kq[C:] * decay, 0.0)
            rhs = jnp.concatenate([v * beta, kb * egc], axis=1).astype(BF16)
            k_dec = k * jnp.exp(g_last - gc_col)
            wq_s[h, c, C:, :] = (q * egc).astype(BF16)
            ka_s[h, c, :Dh, :] = k_dec.T.astype(BF16)
            ka_s[h, c, Dh:, :] = attn.astype(BF16)
            gl_s[h, c] = jnp.broadcast_to(jnp.exp(g_last), (8, Dh))
            staged.append((h, c, r0, L, rhs))
        tinvs = _unit_lower_inverses([s[3] for s in staged])
        for (h, c, r0, _, rhs), tinv in zip(staged, tinvs):
            uw = _dot(tinv.astype(BF16), rhs)
            u_s[h, pl.ds(r0, C), :] = uw[:, :Dh]
            wq_s[h, c, :C, :] = uw[:, Dh:].astype(BF16)

    def recur(i):
        for cc in range(GDN_PREP_CHUNKS):
            c = i * GDN_PREP_CHUNKS + cc
            r0 = c * C if isinstance(c, int) else pl.multiple_of(c * C, C)
            for h in range(HB):
                S = state_s[h]
                r = _dot(wq_s[h, c], S.astype(BF16))
                v_new = u_s[h, pl.ds(r0, C), :] - r[:C]
                r2 = _dot(ka_s[h, c], v_new.astype(BF16))
                state_s[h] = S * gl_s[h, c][0:1, :] + r2[:Dh]
                o = r[C:] + r2[Dh:]
                z = z_ref[pl.ds(r0, C), h * Dh:(h + 1) * Dh]
                on = o * lax.rsqrt(jnp.mean(o * o, axis=-1, keepdims=True) + EPS) * og_ref[...]
                o_ref[pl.ds(r0, C), h * Dh:(h + 1) * Dh] = (on * _silu(z)).astype(o_ref.dtype)

    n_groups = NC // GDN_PREP_CHUNKS
    prepare(0)

    def body(i, carry):
        recur(i)
        prepare(i + 1)
        return carry

    lax.fori_loop(0, n_groups - 1, body, 0)
    recur(n_groups - 1)


def _gdn_call(proj, bg, o_gain):
    B, T, _ = proj.shape
    H, Dh, HB, TB, C = GDN_HEADS, GDN_HEAD_DIM, GDN_HB, GDN_TB, CHUNK
    G = H // HB
    Wb = HB * Dh
    NC = TB // C
    col = lambda off: pl.BlockSpec((None, TB, Wb), lambda b, g, t: (b, t, off + g))
    return pl.pallas_call(
        _gdn_kernel,
        grid=(B, G, T // TB),
        in_specs=[col(0), col(G), col(2 * G), col(3 * G),
                  pl.BlockSpec((None, TB, LANES), lambda b, g, t: (b, t, 0)),
                  pl.BlockSpec((1, Dh), lambda b, g, t: (0, 0))],
        out_specs=pl.BlockSpec((None, TB, Wb), lambda b, g, t: (b, t, g)),
        out_shape=jax.ShapeDtypeStruct((B, T, H * Dh), BF16),
        scratch_shapes=[pltpu.VMEM((HB, Dh, Dh), F32),
                        pltpu.VMEM((HB, TB, 1), F32),
                        pltpu.VMEM((HB, TB, 1), F32),
                        pltpu.VMEM((HB, TB, Dh), F32),
                        pltpu.VMEM((HB, NC, 2 * C, Dh), BF16),
                        pltpu.VMEM((HB, NC, Dh + C, C), BF16),
                        pltpu.VMEM((HB, NC, 8, Dh), F32)],
        compiler_params=_params(3),
        name="gated_delta",
    )(proj, proj, proj, proj, bg, o_gain.reshape(1, Dh))


SB_Q_PER_STEP = 8
SB_WINDOW = 3 * Q_BLOCK
LOG2_E = 1.4426950408889634
LOG2_TINY_F32 = -126.0
MASKED_LOG2 = -1e30


def _sb_log2_terms(l2):
    neg, pos = jnp.minimum(l2, 0.0), jnp.maximum(l2, 0.0)
    t2 = jnp.log2(1.0 + jnp.exp2(neg - pos))
    return neg - t2, pos + t2


def _suffix_sums(drop, later):
    r = drop.shape[0]
    s = _dot(jnp.concatenate(_split2(drop), axis=0), later)
    return s[:r] + s[r:]


def _sb_kernel(q_ref, k_ref, v_ref, z_ref, qg_ref, later_ref, o_ref):
    step = pl.program_id(2)
    QB, Dh, WK = Q_BLOCK, SB_HEAD_DIM, SB_WINDOW
    NW = WK // QB
    WA = WK - QB
    R = 2 * QB
    diag = (lax.broadcasted_iota(jnp.int32, (R, WK), 1)
            - (lax.broadcasted_iota(jnp.int32, (R, WK), 0) & (QB - 1)))
    low = lax.broadcasted_iota(jnp.int32, (QB, LANES), 1) < Dh
    first_q = step * SB_Q_PER_STEP

    pre, qms = [], []
    for qq in range(SB_Q_PER_STEP):
        qi = first_q + qq
        q2 = q_ref[qq * QB:(qq + 1) * QB, :]
        sq = q2 * q2
        s_lo = jnp.sum(jnp.where(low, sq, 0.0), axis=-1, keepdims=True)
        s_hi = jnp.sum(jnp.where(low, 0.0, sq), axis=-1, keepdims=True)
        ms = jnp.where(low, s_lo, s_hi) * (1.0 / Dh)
        qn = q2 * lax.rsqrt(ms + EPS) * qg_ref[...] * (LOG2_E * Dh ** -0.5)
        qm2 = jnp.concatenate([jnp.where(low, qn, 0.0), jnp.where(low, 0.0, qn)], axis=0).astype(BF16)
        qms.append(qm2)
        w0 = jnp.maximum(qi - (NW - 1), 0)
        k0 = pl.multiple_of(w0 * QB, QB)
        pre.append((k0, qi - w0, _dot_nt(qm2, k_ref[pl.ds(k0, WK), :])))

    in_diag = diag[:, WA:] < (NW - 1) * QB
    diag_penalty = jnp.where(in_diag, 0.0, -MASKED_LOG2)
    diag_keep = jnp.where(in_diag, 1.0, 0.0)
    mid = []
    for qq, (k0, dblk, l2) in enumerate(pre):
        log2_sig, drop = _sb_log2_terms(l2)
        if qq >= NW - 1:
            sig_a, drop_a = log2_sig[:, :WA], drop[:, :WA]
            sig_b = log2_sig[:, WA:] - diag_penalty
            drop_b = drop[:, WA:] * diag_keep
        else:
            mask = diag < dblk * QB
            log2_sig = jnp.where(mask, log2_sig, MASKED_LOG2)
            drop = jnp.where(mask, drop, 0.0)
            sig_a, sig_b, drop_a, drop_b = log2_sig[:, :WA], log2_sig[:, WA:], drop[:, :WA], drop[:, WA:]
        mid.append((k0, sig_a, sig_b, _suffix_sums(drop_a, later_ref[...]),
                    _suffix_sums(drop_b, later_ref[:QB, :QB]),
                    jnp.sum(drop_a, axis=-1, keepdims=True), jnp.sum(drop_b, axis=-1, keepdims=True)))

    accs, runs = [], []
    for k0, sig_a, sig_b, after_a, after_b, tot_a, tot_b in mid:
        w = jnp.exp2(jnp.concatenate([sig_a - (after_a + tot_b), sig_b - after_b], axis=1)).astype(BF16)
        accs.append(_dot(w, v_ref[pl.ds(k0, WK), :]))
        runs.append(tot_a + tot_b)

    def unfinished(run):
        return jnp.min(run) <= -LOG2_TINY_F32

    def earlier_blocks():
        out = []
        for qq in range(SB_Q_PER_STEP):
            def cond(c):
                j, _, run = c
                return jnp.logical_and(j >= 0, unfinished(run))

            def body(c, qm2=qms[qq]):
                j, acc, run = c
                r0 = pl.multiple_of(j * QB, QB)
                log2_sig, drop = _sb_log2_terms(_dot_nt(qm2, k_ref[pl.ds(r0, QB), :]))
                w = jnp.exp2(log2_sig - (_suffix_sums(drop, later_ref[:QB, :QB]) + run)).astype(BF16)
                acc = acc + _dot(w, v_ref[pl.ds(r0, QB), :])
                return j - 1, acc, run + jnp.sum(drop, axis=-1, keepdims=True)

            out.append(lax.while_loop(cond, body, (first_q + qq - NW, accs[qq], runs[qq]))[1])
        return out

    slowest = functools.reduce(jnp.minimum, runs)
    any_left = jnp.logical_and(first_q + SB_Q_PER_STEP - 1 >= NW, unfinished(slowest))
    accs = lax.cond(any_left, earlier_blocks, lambda: accs)
    for qq, acc in enumerate(accs):
        rows = slice(qq * QB, (qq + 1) * QB)
        o_ref[rows, :] = (jnp.where(low, acc[:QB], acc[QB:]) * _silu(z_ref[rows, :])).astype(o_ref.dtype)


def _sb_call(proj, kv, q_gain):
    B, T, W2 = proj.shape
    W = W2 // 2
    nb = W // LANES
    tq = SB_Q_PER_STEP * Q_BLOCK
    wa = SB_WINDOW - Q_BLOCK
    qg = jnp.tile(q_gain, LANES // SB_HEAD_DIM).reshape(1, LANES)
    later = jnp.tri(wa, k=-1, dtype=BF16)
    return pl.pallas_call(
        _sb_kernel,
        grid=(B, nb, T // tq),
        in_specs=[pl.BlockSpec((None, tq, LANES), lambda b, p, i: (b, i, p)),
                  pl.BlockSpec((None, T, LANES), lambda b, p, i: (b, 0, p)),
                  pl.BlockSpec((None, T, LANES), lambda b, p, i: (b, 0, nb + p)),
                  pl.BlockSpec((None, tq, LANES), lambda b, p, i: (b, i, nb + p)),
                  pl.BlockSpec((1, LANES), lambda b, p, i: (0, 0)),
                  pl.BlockSpec((wa, wa), lambda b, p, i: (0, 0))],
        out_specs=pl.BlockSpec((None, tq, LANES), lambda b, p, i: (b, i, p)),
        out_shape=jax.ShapeDtypeStruct((B, T, W), BF16),
        compiler_params=_params(3),
        name="stick_breaking",
    )(proj, kv, kv, proj, qg, later)


def kernel(x, c, norm_g, w_ada, b_ada, w_in_a, conv_w_a, a_log_a, dt_bias_a, o_gain_a, w_out_a,
           kv_norm_g, w_ada_kv, b_ada_kv, w_kv, k_gain, w_in_b, q_gain_b, w_out_b):
    B, T, D = x.shape
    depth = w_ada.shape[0]
    n_a = w_in_a.shape[0]
    gw = GDN_HEADS * GDN_HEAD_DIM

    mod4 = _mod_call(c, w_ada, b_ada).reshape(depth, B, 1, 3 * D)
    modkv4 = _mod_call(c, w_ada_kv[None], b_ada_kv[None]).reshape(1, B, 1, 2 * D)

    kv = None
    for layer in range(depth):
        if layer < n_a:
            i = layer
            w_main = w_in_a[i, :, :4 * gw].astype(BF16)
            w_tail = jnp.pad(w_in_a[i, :, 4 * gw:], ((0, 0), (0, LANES - 2 * GDN_HEADS))).astype(BF16)
            gate_row = lambda vec: jnp.zeros((1, LANES), F32).at[0, GDN_HEADS:2 * GDN_HEADS].set(vec)
            proj, bg = _norm_proj_call(
                _gdn_in_proj_kernel, x, norm_g[layer], mod4, layer, [w_main, w_tail],
                extra=(conv_w_a[i], gate_row(a_log_a[i]), gate_row(dt_bias_a[i])), name="gdn_in_proj",
                scratch_shapes=[pltpu.VMEM((HALO + ROW_TILE, 3 * gw), F32)])
            o = _gdn_call(proj, bg, o_gain_a[i])
            x = _outproj_call(o, w_out_a[i].astype(BF16), x, mod4, layer)
        else:
            if kv is None:
                kg = jnp.tile(k_gain, LANES // SB_HEAD_DIM).reshape(1, LANES)
                (kv,) = _norm_proj_call(_norm_kv_kernel, x, kv_norm_g, modkv4, 0,
                                        [w_kv.astype(BF16)], extra=(kg,), name="shared_kv",
                                        out_dtype=BF16, tm=WIDE_ROW_TILE)
            i = layer - n_a
            (proj,) = _norm_proj_call(_norm_proj_kernel, x, norm_g[layer], mod4, layer,
                                      [w_in_b[i].astype(BF16)], name="sb_in_proj", tm=WIDE_ROW_TILE)
            o = _sb_call(proj, kv, q_gain_b[i])
            x = _outproj_call(o, w_out_b[i].astype(BF16), x, mod4, layer)
    return x
```

```python
import functools

import jax
import jax.numpy as jnp
from jax import lax
from jax.experimental import pallas as pl
from jax.experimental.pallas import tpu as pltpu

F32 = jnp.float32
BF16 = jnp.bfloat16
HI = lax.Precision.HIGHEST

EPS = 1e-6
CHUNK = 64
GDN_HEADS = 8
GDN_HEAD_DIM = 128
CONV_WIDTH = 4
SB_HEADS = 16
SB_HEAD_DIM = 64
Q_BLOCK = 128
LANES = 128
VMEM_LIMIT = 48 * 1024 * 1024
ROW_TILE = 256
WIDE_ROW_TILE = 1024
N_CHUNK = 512
HALO = 8


def _params(n_axes):
    return pltpu.CompilerParams(dimension_semantics=("arbitrary",) * n_axes,
                                vmem_limit_bytes=VMEM_LIMIT)


def _silu(x):
    return x * jax.nn.sigmoid(x)


def _dot(a, b, precision=None):
    return jnp.dot(a, b, precision=precision, preferred_element_type=F32)


def _dot_nt(a, b, precision=None):
    return lax.dot_general(a, b, (((1,), (1,)), ((), ())), precision=precision,
                           preferred_element_type=F32)


def _mod_kernel(c_ref, w_ref, b_ref, o_ref):
    ca = _silu(c_ref[...])
    o_ref[...] = _dot(ca, w_ref[...], HI) + b_ref[...]


def _mod_call(c, w, b):
    L, D, N = w.shape
    B = c.shape[0]
    tn = 1024
    return pl.pallas_call(
        _mod_kernel,
        grid=(L, N // tn),
        in_specs=[pl.BlockSpec((B, D), lambda l, j: (0, 0)),
                  pl.BlockSpec((None, D, tn), lambda l, j: (l, 0, j)),
                  pl.BlockSpec((None, 1, tn), lambda l, j: (l, 0, j))],
        out_specs=pl.BlockSpec((None, B, tn), lambda l, j: (l, 0, j)),
        out_shape=jax.ShapeDtypeStruct((L, B, N), F32),
        compiler_params=_params(2),
        name="adaln_mod",
    )(c, w, b.reshape(L, 1, N))


def _modnorm(x_ref, g_ref, sh_ref, sc_ref):
    x = x_ref[...]
    n = x * lax.rsqrt(jnp.mean(x * x, axis=-1, keepdims=True) + EPS) * g_ref[...]
    return (n * (1.0 + sc_ref[...]) + sh_ref[...]).astype(BF16)


def _proj_cols(hb, w_ref, o_ref):
    n = w_ref.shape[1]
    for j in range(0, n, N_CHUNK):
        o_ref[:, j:j + N_CHUNK] = _dot(hb, w_ref[:, j:j + N_CHUNK])


def _norm_proj_kernel(x_ref, g_ref, sh_ref, sc_ref, w_ref, o_ref):
    _proj_cols(_modnorm(x_ref, g_ref, sh_ref, sc_ref), w_ref, o_ref)


def _gdn_in_proj_kernel(x_ref, g_ref, sh_ref, sc_ref, w_ref, wt_ref, cw_ref, alog_ref, dtb_ref,
                        o_ref, bg_ref, xe_s):
    tm = x_ref.shape[0]
    Dh = GDN_HEAD_DIM
    w3 = cw_ref.shape[1]
    wd = w3 // 3

    @pl.when(pl.program_id(1) == 0)
    def _():
        xe_s[:HALO, :] = jnp.zeros((HALO, w3), F32)

    hb = _modnorm(x_ref, g_ref, sh_ref, sc_ref)
    for j in range(0, w3, N_CHUNK):
        cols = slice(j, j + N_CHUNK)
        raw = _dot(hb, w_ref[:, cols])
        xe_s[HALO:, cols] = raw
        cw = cw_ref[:, cols]
        acc = raw * cw[CONV_WIDTH - 1:CONV_WIDTH, :]
        for s in range(1, CONV_WIDTH):
            acc = acc + xe_s[HALO - s:HALO - s + tm, cols] * cw[CONV_WIDTH - 1 - s:CONV_WIDTH - s, :]
        xe_s[:HALO, cols] = raw[tm - HALO:]
        y = _silu(acc)
        if j < 2 * wd:
            scale = Dh ** -0.5 if j < wd else 1.0
            for h0 in range(0, N_CHUNK, Dh):
                yh = y[:, h0:h0 + Dh]
                inv = lax.rsqrt(jnp.sum(yh * yh, axis=-1, keepdims=True) + EPS) * scale
                o_ref[:, j + h0:j + h0 + Dh] = yh * inv
        else:
            o_ref[:, cols] = y
    for j in range(w3, w_ref.shape[1], N_CHUNK):
        o_ref[:, j:j + N_CHUNK] = _dot(hb, w_ref[:, j:j + N_CHUNK])
    t = _dot(hb, wt_ref[...])
    a = t + dtb_ref[...]
    g = -jnp.exp(alog_ref[...]) * (jnp.maximum(a, 0.0) + jnp.log1p(jnp.exp(-jnp.abs(a))))
    lane = lax.broadcasted_iota(jnp.int32, t.shape, 1)
    bg_ref[...] = jnp.where(lane < GDN_HEADS, jax.nn.sigmoid(t), g)


def _norm_kv_kernel(x_ref, g_ref, sh_ref, sc_ref, w_ref, kg_ref, o_ref):
    hb = _modnorm(x_ref, g_ref, sh_ref, sc_ref)
    n = w_ref.shape[1]
    half = n // 2
    tm = x_ref.shape[0]
    low = lax.broadcasted_iota(jnp.int32, (tm, LANES), 1) < SB_HEAD_DIM
    for j in range(0, half, LANES):
        kk = _dot(hb, w_ref[:, j:j + LANES])
        sq = kk * kk
        s_lo = jnp.sum(jnp.where(low, sq, 0.0), axis=-1, keepdims=True)
        s_hi = jnp.sum(jnp.where(low, 0.0, sq), axis=-1, keepdims=True)
        ms = jnp.where(low, s_lo, s_hi) * (1.0 / SB_HEAD_DIM)
        o_ref[:, j:j + LANES] = (kk * lax.rsqrt(ms + EPS) * kg_ref[...]).astype(o_ref.dtype)
    for j in range(half, n, N_CHUNK):
        o_ref[:, j:j + N_CHUNK] = _dot(hb, w_ref[:, j:j + N_CHUNK]).astype(o_ref.dtype)


def _norm_proj_call(kernel_fn, x, gain, mod4, layer, w_list, extra=(), name="norm_proj",
                    out_dtype=F32, scratch_shapes=(), tm=ROW_TILE):
    B, T, D = x.shape
    in_specs = [pl.BlockSpec((None, tm, D), lambda b, i: (b, i, 0)),
                pl.BlockSpec((1, D), lambda b, i: (0, 0)),
                pl.BlockSpec((None, None, 1, D), lambda b, i: (layer, b, 0, 0)),
                pl.BlockSpec((None, None, 1, D), lambda b, i: (layer, b, 0, 1))]
    out_specs, out_shape = [], []
    for w in w_list:
        n = w.shape[1]
        in_specs.append(pl.BlockSpec((D, n), lambda b, i: (0, 0)))
        out_specs.append(pl.BlockSpec((None, tm, n), lambda b, i: (b, i, 0)))
        out_shape.append(jax.ShapeDtypeStruct((B, T, n), out_dtype))
    for e in extra:
        in_specs.append(pl.BlockSpec(e.shape, lambda b, i: (0, 0)))
    return pl.pallas_call(
        kernel_fn,
        grid=(B, T // tm),
        in_specs=in_specs,
        out_specs=out_specs,
        out_shape=out_shape,
        scratch_shapes=list(scratch_shapes),
        compiler_params=_params(2),
        name=name,
    )(x, gain.reshape(1, D), mod4, mod4, *w_list, *extra)


def _outproj_kernel(o_ref, w_ref, x_ref, gate_ref, xo_ref):
    y = _dot(o_ref[...], w_ref[...])
    xo_ref[...] = x_ref[...] + gate_ref[...] * y


def _outproj_call(o, w, x, mod4, layer):
    B, T, D = x.shape
    W = o.shape[-1]
    tm = WIDE_ROW_TILE
    return pl.pallas_call(
        _outproj_kernel,
        grid=(B, T // tm),
        in_specs=[pl.BlockSpec((None, tm, W), lambda b, i: (b, i, 0)),
                  pl.BlockSpec((W, D), lambda b, i: (0, 0)),
                  pl.BlockSpec((None, tm, D), lambda b, i: (b, i, 0)),
                  pl.BlockSpec((None, None, 1, D), lambda b, i: (layer, b, 0, 2))],
        out_specs=pl.BlockSpec((None, tm, D), lambda b, i: (b, i, 0)),
        out_shape=jax.ShapeDtypeStruct((B, T, D), F32),
        compiler_params=_params(2),
        name="outproj_residual",
    )(o, w, x, mod4)


GDN_HB = 8
GDN_TB = 512
GDN_PREP_CHUNKS = 2


def _split2(x):
    hi = x.astype(BF16)
    return hi, (x - hi.astype(F32)).astype(BF16)


def _block_diag_pair(p):
    left = lax.broadcasted_iota(jnp.int32, p.shape, 1) < p.shape[0]
    zero = jnp.zeros_like(p)
    return jnp.concatenate([jnp.where(left, p, zero), jnp.where(left, zero, p)], axis=0)


def _dot_split_pairs(lhs_parts, p_parts):
    n = p_parts[0].shape[0]
    m = len(lhs_parts)
    his, los = zip(*lhs_parts)
    r1 = _dot(jnp.concatenate(his + los, axis=0), _block_diag_pair(p_parts[0]))
    r2 = _dot(jnp.concatenate(his, axis=0), _block_diag_pair(p_parts[1]))
    return [r1[i * n:(i + 1) * n] + r1[(m + i) * n:(m + i + 1) * n] + r2[i * n:(i + 1) * n]
            for i in range(m)]


def _unit_lower_inverses(Ls):
    n = Ls[0].shape[0]
    pairs = [jnp.concatenate([a, b], axis=1) for a, b in zip(Ls[0::2], Ls[1::2])]
    ri = lax.broadcasted_iota(jnp.int32, (n, 2 * n), 0)
    ci = lax.broadcasted_iota(jnp.int32, (n, 2 * n), 1) & (n - 1)
    eye = jnp.where(ri == ci, 1.0, 0.0)
    Xs = [eye - L for L in pairs]
    Ps = []
    for L in pairs:
        parts = _split2(L)
        Ps.append(_dot_split_pairs([parts], parts)[0])
    k = 4
    while k < n:
        outs = []
        for X, P in zip(Xs, Ps):
            p_parts = _split2(P)
            outs.append(_dot_split_pairs([_split2(X), p_parts], p_parts))
        Xs = [X + o[0] for X, o in zip(Xs, outs)]
        Ps = [o[1] for o in outs]
        k *= 2
    out = []
    for X, P in zip(Xs, Ps):
        T = X + _dot_split_pairs([_split2(X)], _split2(P))[0]
        out += [T[:, :n], T[:, n:]]
    return out


def _gdn_kernel(q_ref, k_ref, v_ref, z_ref, bg_ref, og_ref, o_ref,
                state_s, beta_s, g_s, u_s, wq_s, ka_s, gl_s):
    hg = pl.program_id(1)
    tb = pl.program_id(2)
    TB = q_ref.shape[0]
    HB, Dh, C = GDN_HB, GDN_HEAD_DIM, CHUNK
    NC = TB // C

    @pl.when(tb == 0)
    def _():
        state_s[...] = jnp.zeros_like(state_s)

    lane = lax.broadcasted_iota(jnp.int32, (TB, LANES), 1)
    bg = bg_ref[...]
    for h in range(HB):
        head = hg * HB + h
        beta_s[h] = jnp.sum(jnp.where(lane == head, bg, 0.0), axis=-1, keepdims=True)
        g_s[h] = jnp.sum(jnp.where(lane == head + GDN_HEADS, bg, 0.0), axis=-1, keepdims=True)

    ri = lax.broadcasted_iota(jnp.int32, (C, C), 0)
    ci = lax.broadcasted_iota(jnp.int32, (C, C), 1)
    tril = ri >= ci
    strict = ri > ci
    eye = ri == ci

    def to_row(col):
        return jnp.sum(jnp.where(eye, jnp.broadcast_to(col, (C, C)), 0.0), axis=0, keepdims=True)

    def prepare(i):
        chains = [(h, i * GDN_PREP_CHUNKS + cc) for cc in range(GDN_PREP_CHUNKS) for h in range(HB)]
        staged = []
        for h, c in chains:
            r0 = c * C if isinstance(c, int) else pl.multiple_of(c * C, C)
            q = q_ref[pl.ds(r0, C), h * Dh:(h + 1) * Dh]
            k = k_ref[pl.ds(r0, C), h * Dh:(h + 1) * Dh]
            v = v_ref[pl.ds(r0, C), h * Dh:(h + 1) * Dh]
            beta = beta_s[h, pl.ds(r0, C), :]
            g_row = to_row(g_s[h, pl.ds(r0, C), :])
            gc_col = jnp.sum(jnp.where(tril, jnp.broadcast_to(g_row, (C, C)), 0.0),
                             axis=1, keepdims=True)
            gc_row = to_row(gc_col)
            g_last = gc_col[C - 1:C, :]
            decay = jnp.where(tril, jnp.exp(jnp.where(tril, gc_col - gc_row, 0.0)), 0.0)
            kb = k * beta
            egc = jnp.exp(gc_col)
            kq = _dot_nt(jnp.concatenate([kb, q], axis=0).astype(BF16), k.astype(BF16))
            L = jnp.where(strict, kq[:C] * decay, 0.0)
            attn = jnp.where(tril, kq[C:] * decay, 0.0)
            rhs = jnp.concatenate([v * beta, kb * egc], axis=1).astype(BF16)
            k_dec = k * jnp.exp(g_last - gc_col)
            wq_s[h, c, C:, :] = (q * egc).astype(BF16)
            ka_s[h, c, :Dh, :] = k_dec.T.astype(BF16)
            ka_s[h, c, Dh:, :] = attn.astype(BF16)
            gl_s[h, c] = jnp.broadcast_to(jnp.exp(g_last), (8, Dh))
            staged.append((h, c, r0, L, rhs))
        tinvs = _unit_lower_inverses([s[3] for s in staged])
        for (h, c, r0, _, rhs), tinv in zip(staged, tinvs):
            uw = _dot(tinv.astype(BF16), rhs)
            u_s[h, pl.ds(r0, C), :] = uw[:, :Dh]
            wq_s[h, c, :C, :] = uw[:, Dh:].astype(BF16)

    def recur(i):
        for cc in range(GDN_PREP_CHUNKS):
            c = i * GDN_PREP_CHUNKS + cc
            r0 = c * C if isinstance(c, int) else pl.multiple_of(c * C, C)
            for h in range(HB):
                S = state_s[h]
                r = _dot(wq_s[h, c], S.astype(BF16))
                v_new = u_s[h, pl.ds(r0, C), :] - r[:C]
                r2 = _dot(ka_s[h, c], v_new.astype(BF16))
                state_s[h] = S * gl_s[h, c][0:1, :] + r2[:Dh]
                o = r[C:] + r2[Dh:]
                z = z_ref[pl.ds(r0, C), h * Dh:(h + 1) * Dh]
                on = o * lax.rsqrt(jnp.mean(o * o, axis=-1, keepdims=True) + EPS) * og_ref[...]
                o_ref[pl.ds(r0, C), h * Dh:(h + 1) * Dh] = (on * _silu(z)).astype(o_ref.dtype)

    n_groups = NC // GDN_PREP_CHUNKS
    prepare(0)

    def body(i, carry):
        recur(i)
        prepare(i + 1)
        return carry

    lax.fori_loop(0, n_groups - 1, body, 0)
    recur(n_groups - 1)


def _gdn_call(proj, bg, o_gain):
    B, T, _ = proj.shape
    H, Dh, HB, TB, C = GDN_HEADS, GDN_HEAD_DIM, GDN_HB, GDN_TB, CHUNK
    G = H // HB
    Wb = HB * Dh
    NC = TB // C
    col = lambda off: pl.BlockSpec((None, TB, Wb), lambda b, g, t: (b, t, off + g))
    return pl.pallas_call(
        _gdn_kernel,
        grid=(B, G, T // TB),
        in_specs=[col(0), col(G), col(2 * G), col(3 * G),
                  pl.BlockSpec((None, TB, LANES), lambda b, g, t: (b, t, 0)),
                  pl.BlockSpec((1, Dh), lambda b, g, t: (0, 0))],
        out_specs=pl.BlockSpec((None, TB, Wb), lambda b, g, t: (b, t, g)),
        out_shape=jax.ShapeDtypeStruct((B, T, H * Dh), BF16),
        scratch_shapes=[pltpu.VMEM((HB, Dh, Dh), F32),
                        pltpu.VMEM((HB, TB, 1), F32),
                        pltpu.VMEM((HB, TB, 1), F32),
                        pltpu.VMEM((HB, TB, Dh), F32),
                        pltpu.VMEM((HB, NC, 2 * C, Dh), BF16),
                        pltpu.VMEM((HB, NC, Dh + C, C), BF16),
                        pltpu.VMEM((HB, NC, 8, Dh), F32)],
        compiler_params=_params(3),
        name="gated_delta",
    )(proj, proj, proj, proj, bg, o_gain.reshape(1, Dh))


SB_Q_PER_STEP = 8
SB_WINDOW = 3 * Q_BLOCK
LOG2_E = 1.4426950408889634
LOG2_TINY_F32 = -126.0
MASKED_LOG2 = -1e30


def _sb_log2_terms(l2):
    neg, pos = jnp.minimum(l2, 0.0), jnp.maximum(l2, 0.0)
    t2 = jnp.log2(1.0 + jnp.exp2(neg - pos))
    return neg - t2, pos + t2


def _suffix_sums(drop, later):
    r = drop.shape[0]
    s = _dot(jnp.concatenate(_split2(drop), axis=0), later)
    return s[:r] + s[r:]


def _sb_kernel(q_ref, k_ref, v_ref, z_ref, qg_ref, later_ref, o_ref):
    step = pl.program_id(2)
    QB, Dh, WK = Q_BLOCK, SB_HEAD_DIM, SB_WINDOW
    NW = WK // QB
    WA = WK - QB
    R = 2 * QB
    diag = (lax.broadcasted_iota(jnp.int32, (R, WK), 1)
            - (lax.broadcasted_iota(jnp.int32, (R, WK), 0) & (QB - 1)))
    low = lax.broadcasted_iota(jnp.int32, (QB, LANES), 1) < Dh
    first_q = step * SB_Q_PER_STEP

    pre, qms = [], []
    for qq in range(SB_Q_PER_STEP):
        qi = first_q + qq
        q2 = q_ref[qq * QB:(qq + 1) * QB, :]
        sq = q2 * q2
        s_lo = jnp.sum(jnp.where(low, sq, 0.0), axis=-1, keepdims=True)
        s_hi = jnp.sum(jnp.where(low, 0.0, sq), axis=-1, keepdims=True)
        ms = jnp.where(low, s_lo, s_hi) * (1.0 / Dh)
        qn = q2 * lax.rsqrt(ms + EPS) * qg_ref[...] * (LOG2_E * Dh ** -0.5)
        qm2 = jnp.concatenate([jnp.where(low, qn, 0.0), jnp.where(low, 0.0, qn)], axis=0).astype(BF16)
        qms.append(qm2)
        w0 = jnp.maximum(qi - (NW - 1), 0)
        k0 = pl.multiple_of(w0 * QB, QB)
        pre.append((k0, qi - w0, _dot_nt(qm2, k_ref[pl.ds(k0, WK), :])))

    in_diag = diag[:, WA:] < (NW - 1) * QB
    diag_penalty = jnp.where(in_diag, 0.0, -MASKED_LOG2)
    diag_keep = jnp.where(in_diag, 1.0, 0.0)
    mid = []
    for qq, (k0, dblk, l2) in enumerate(pre):
        log2_sig, drop = _sb_log2_terms(l2)
        if qq >= NW - 1:
            sig_a, drop_a = log2_sig[:, :WA], drop[:, :WA]
            sig_b = log2_sig[:, WA:] - diag_penalty
            drop_b = drop[:, WA:] * diag_keep
        else:
            mask = diag < dblk * QB
            log2_sig = jnp.where(mask, log2_sig, MASKED_LOG2)
            drop = jnp.where(mask, drop, 0.0)
            sig_a, sig_b, drop_a, drop_b = log2_sig[:, :WA], log2_sig[:, WA:], drop[:, :WA], drop[:, WA:]
        mid.append((k0, sig_a, sig_b, _suffix_sums(drop_a, later_ref[...]),
                    _suffix_sums(drop_b, later_ref[:QB, :QB]),
                    jnp.sum(drop_a, axis=-1, keepdims=True), jnp.sum(drop_b, axis=-1, keepdims=True)))

    accs, runs = [], []
    for k0, sig_a, sig_b, after_a, after_b, tot_a, tot_b in mid:
        w = jnp.exp2(jnp.concatenate([sig_a - (after_a + tot_b), sig_b - after_b], axis=1)).astype(BF16)
        accs.append(_dot(w, v_ref[pl.ds(k0, WK), :]))
        runs.append(tot_a + tot_b)

    def unfinished(run):
        return jnp.min(run) <= -LOG2_TINY_F32

    def earlier_blocks():
        out = []
        for qq in range(SB_Q_PER_STEP):
            def cond(c):
                j, _, run = c
                return jnp.logical_and(j >= 0, unfinished(run))

            def body(c, qm2=qms[qq]):
                j, acc, run = c
                r0 = pl.multiple_of(j * QB, QB)
                log2_sig, drop = _sb_log2_terms(_dot_nt(qm2, k_ref[pl.ds(r0, QB), :]))
                w = jnp.exp2(log2_sig - (_suffix_sums(drop, later_ref[:QB, :QB]) + run)).astype(BF16)
                acc = acc + _dot(w, v_ref[pl.ds(r0, QB), :])
                return j - 1, acc, run + jnp.sum(drop, axis=-1, keepdims=True)

            out.append(lax.while_loop(cond, body, (first_q + qq - NW, accs[qq], runs[qq]))[1])
        return out

    slowest = functools.reduce(jnp.minimum, runs)
    any_left = jnp.logical_and(first_q + SB_Q_PER_STEP - 1 >= NW, unfinished(slowest))
    accs = lax.cond(any_left, earlier_blocks, lambda: accs)
    for qq, acc in enumerate(accs):
        rows = slice(qq * QB, (qq + 1) * QB)
        o_ref[rows, :] = (jnp.where(low, acc[:QB], acc[QB:]) * _silu(z_ref[rows, :])).astype(o_ref.dtype)


def _sb_call(proj, kv, q_gain):
    B, T, W2 = proj.shape
    W = W2 // 2
    nb = W // LANES
    tq = SB_Q_PER_STEP * Q_BLOCK
    wa = SB_WINDOW - Q_BLOCK
    qg = jnp.tile(q_gain, LANES // SB_HEAD_DIM).reshape(1, LANES)
    later = jnp.tri(wa, k=-1, dtype=BF16)
    return pl.pallas_call(
        _sb_kernel,
        grid=(B, nb, T // tq),
        in_specs=[pl.BlockSpec((None, tq, LANES), lambda b, p, i: (b, i, p)),
                  pl.BlockSpec((None, T, LANES), lambda b, p, i: (b, 0, p)),
                  pl.BlockSpec((None, T, LANES), lambda b, p, i: (b, 0, nb + p)),
                  pl.BlockSpec((None, tq, LANES), lambda b, p, i: (b, i, nb + p)),
                  pl.BlockSpec((1, LANES), lambda b, p, i: (0, 0)),
                  pl.BlockSpec((wa, wa), lambda b, p, i: (0, 0))],
        out_specs=pl.BlockSpec((None, tq, LANES), lambda b, p, i: (b, i, p)),
        out_shape=jax.ShapeDtypeStruct((B, T, W), BF16),
        compiler_params=_params(3),
        name="stick_breaking",
    )(proj, kv, kv, proj, qg, later)


def kernel(x, c, norm_g, w_ada, b_ada, w_in_a, conv_w_a, a_log_a, dt_bias_a, o_gain_a, w_out_a,
           kv_norm_g, w_ada_kv, b_ada_kv, w_kv, k_gain, w_in_b, q_gain_b, w_out_b):
    B, T, D = x.shape
    depth = w_ada.shape[0]
    n_a = w_in_a.shape[0]
    gw = GDN_HEADS * GDN_HEAD_DIM

    mod4 = _mod_call(c, w_ada, b_ada).reshape(depth, B, 1, 3 * D)
    modkv4 = _mod_call(c, w_ada_kv[None], b_ada_kv[None]).reshape(1, B, 1, 2 * D)

    kv = None
    for layer in range(depth):
        if layer < n_a:
            i = layer
            w_main = w_in_a[i, :, :4 * gw].astype(BF16)
            w_tail = jnp.pad(w_in_a[i, :, 4 * gw:], ((0, 0), (0, LANES - 2 * GDN_HEADS))).astype(BF16)
            gate_row = lambda vec: jnp.zeros((1, LANES), F32).at[0, GDN_HEADS:2 * GDN_HEADS].set(vec)
            proj, bg = _norm_proj_call(
                _gdn_in_proj_kernel, x, norm_g[layer], mod4, layer, [w_main, w_tail],
                extra=(conv_w_a[i], gate_row(a_log_a[i]), gate_row(dt_bias_a[i])), name="gdn_in_proj",
                scratch_shapes=[pltpu.VMEM((HALO + ROW_TILE, 3 * gw), F32)])
            o = _gdn_call(proj, bg, o_gain_a[i])
            x = _outproj_call(o, w_out_a[i].astype(BF16), x, mod4, layer)
        else:
            if kv is None:
                kg = jnp.tile(k_gain, LANES // SB_HEAD_DIM).reshape(1, LANES)
                (kv,) = _norm_proj_call(_norm_kv_kernel, x, kv_norm_g, modkv4, 0,
                                        [w_kv.astype(BF16)], extra=(kg,), name="shared_kv",
                                        out_dtype=BF16, tm=WIDE_ROW_TILE)
            i = layer - n_a
            (proj,) = _norm_proj_call(_norm_proj_kernel, x, norm_g[layer], mod4, layer,
                                      [w_in_b[i].astype(BF16)], name="sb_in_proj", tm=WIDE_ROW_TILE)
            o = _sb_call(proj, kv, q_gain_b[i])
            x = _outproj_call(o, w_out_b[i].astype(BF16), x, mod4, layer)
    return x
```

```python
import functools

import jax
import jax.numpy as jnp
from jax import lax
from jax.experimental import pallas as pl
from jax.experimental.pallas import tpu as pltpu

F32 = jnp.float32
BF16 = jnp.bfloat16
HI = lax.Precision.HIGHEST

EPS = 1e-6
CHUNK = 64
GDN_HEADS = 8
GDN_HEAD_DIM = 128
CONV_WIDTH = 4
SB_HEADS = 16
SB_HEAD_DIM = 64
Q_BLOCK = 128
LANES = 128
VMEM_LIMIT = 48 * 1024 * 1024
ROW_TILE = 256
WIDE_ROW_TILE = 1024
N_CHUNK = 512
HALO = 8


def _params(n_axes):
    return pltpu.CompilerParams(dimension_semantics=("arbitrary",) * n_axes,
                                vmem_limit_bytes=VMEM_LIMIT)


def _silu(x):
    return x * jax.nn.sigmoid(x)


def _dot(a, b, precision=None):
    return jnp.dot(a, b, precision=precision, preferred_element_type=F32)


def _dot_nt(a, b, precision=None):
    return lax.dot_general(a, b, (((1,), (1,)), ((), ())), precision=precision,
                           preferred_element_type=F32)


def _mod_kernel(c_ref, w_ref, b_ref, o_ref):
    ca = _silu(c_ref[...])
    o_ref[...] = _dot(ca, w_ref[...], HI) + b_ref[...]


def _mod_call(c, w, b):
    L, D, N = w.shape
    B = c.shape[0]
    tn = 1024
    return pl.pallas_call(
        _mod_kernel,
        grid=(L, N // tn),
        in_specs=[pl.BlockSpec((B, D), lambda l, j: (0, 0)),
                  pl.BlockSpec((None, D, tn), lambda l, j: (l, 0, j)),
                  pl.BlockSpec((None, 1, tn), lambda l, j: (l, 0, j))],
        out_specs=pl.BlockSpec((None, B, tn), lambda l, j: (l, 0, j)),
        out_shape=jax.ShapeDtypeStruct((L, B, N), F32),
        compiler_params=_params(2),
        name="adaln_mod",
    )(c, w, b.reshape(L, 1, N))


def _modnorm(x_ref, g_ref, sh_ref, sc_ref):
    x = x_ref[...]
    n = x * lax.rsqrt(jnp.mean(x * x, axis=-1, keepdims=True) + EPS) * g_ref[...]
    return (n * (1.0 + sc_ref[...]) + sh_ref[...]).astype(BF16)


def _proj_cols(hb, w_ref, o_ref):
    n = w_ref.shape[1]
    for j in range(0, n, N_CHUNK):
        o_ref[:, j:j + N_CHUNK] = _dot(hb, w_ref[:, j:j + N_CHUNK])


def _norm_proj_kernel(x_ref, g_ref, sh_ref, sc_ref, w_ref, o_ref):
    _proj_cols(_modnorm(x_ref, g_ref, sh_ref, sc_ref), w_ref, o_ref)


def _gdn_in_proj_kernel(x_ref, g_ref, sh_ref, sc_ref, w_ref, wt_ref, cw_ref, alog_ref, dtb_ref,
                        o_ref, bg_ref, xe_s, ve_s, prev_s):
    assert CONV_WIDTH == 4
    tm = x_ref.shape[0]
    Dh = GDN_HEAD_DIM
    w3 = cw_ref.shape[1]
    wd = w3 // 3

    @pl.when(pl.program_id(1) == 0)
    def _():
        xe_s[:HALO, :] = jnp.zeros((HALO, w3), F32)
        ve_s[:HALO, :] = jnp.zeros((HALO, w3), F32)

    hb = _modnorm(x_ref, g_ref, sh_ref, sc_ref)
    for j in range(0, w3, N_CHUNK):
        cols = slice(j, j + N_CHUNK)
        raw = _dot(hb, w_ref[:, cols])
        xe_s[HALO:, cols] = raw
        prev_s[...] = xe_s[HALO - 1:HALO - 1 + tm, cols]
        v = raw * cw_ref[1:2, cols] + prev_s[...] * cw_ref[0:1, cols]
        ve_s[HALO:, cols] = v
        acc = (raw * cw_ref[3:4, cols] + prev_s[...] * cw_ref[2:3, cols]
               + ve_s[HALO - 2:HALO - 2 + tm, cols])
        xe_s[:HALO, cols] = raw[tm - HALO:]
        ve_s[:HALO, cols] = v[tm - HALO:]
        y = _silu(acc)
        if j < 2 * wd:
            scale = Dh ** -0.5 if j < wd else 1.0
            for h0 in range(0, N_CHUNK, Dh):
                yh = y[:, h0:h0 + Dh]
                inv = lax.rsqrt(jnp.sum(yh * yh, axis=-1, keepdims=True) + EPS) * scale
                o_ref[:, j + h0:j + h0 + Dh] = yh * inv
        else:
            o_ref[:, cols] = y
    for j in range(w3, w_ref.shape[1], N_CHUNK):
        o_ref[:, j:j + N_CHUNK] = _dot(hb, w_ref[:, j:j + N_CHUNK])
    t = _dot(hb, wt_ref[...])
    a = t + dtb_ref[...]
    g = -jnp.exp(alog_ref[...]) * (jnp.maximum(a, 0.0) + jnp.log1p(jnp.exp(-jnp.abs(a))))
    lane = lax.broadcasted_iota(jnp.int32, t.shape, 1)
    bg_ref[...] = jnp.where(lane < GDN_HEADS, jax.nn.sigmoid(t), g)


def _norm_kv_kernel(x_ref, g_ref, sh_ref, sc_ref, w_ref, kg_ref, o_ref):
    hb = _modnorm(x_ref, g_ref, sh_ref, sc_ref)
    n = w_ref.shape[1]
    half = n // 2
    tm = x_ref.shape[0]
    low = lax.broadcasted_iota(jnp.int32, (tm, LANES), 1) < SB_HEAD_DIM
    for j in range(0, half, LANES):
        kk = _dot(hb, w_ref[:, j:j + LANES])
        sq = kk * kk
        s_lo = jnp.sum(jnp.where(low, sq, 0.0), axis=-1, keepdims=True)
        s_hi = jnp.sum(jnp.where(low, 0.0, sq), axis=-1, keepdims=True)
        ms = jnp.where(low, s_lo, s_hi) * (1.0 / SB_HEAD_DIM)
        o_ref[:, j:j + LANES] = (kk * lax.rsqrt(ms + EPS) * kg_ref[...]).astype(o_ref.dtype)
    for j in range(half, n, N_CHUNK):
        o_ref[:, j:j + N_CHUNK] = _dot(hb, w_ref[:, j:j + N_CHUNK]).astype(o_ref.dtype)


def _norm_proj_call(kernel_fn, x, gain, mod4, layer, w_list, extra=(), name="norm_proj",
                    out_dtype=F32, scratch_shapes=(), tm=ROW_TILE):
    B, T, D = x.shape
    in_specs = [pl.BlockSpec((None, tm, D), lambda b, i: (b, i, 0)),
                pl.BlockSpec((1, D), lambda b, i: (0, 0)),
                pl.BlockSpec((None, None, 1, D), lambda b, i: (layer, b, 0, 0)),
                pl.BlockSpec((None, None, 1, D), lambda b, i: (layer, b, 0, 1))]
    out_specs, out_shape = [], []
    for w in w_list:
        n = w.shape[1]
        in_specs.append(pl.BlockSpec((D, n), lambda b, i: (0, 0)))
        out_specs.append(pl.BlockSpec((None, tm, n), lambda b, i: (b, i, 0)))
        out_shape.append(jax.ShapeDtypeStruct((B, T, n), out_dtype))
    for e in extra:
        in_specs.append(pl.BlockSpec(e.shape, lambda b, i: (0, 0)))
    return pl.pallas_call(
        kernel_fn,
        grid=(B, T // tm),
        in_specs=in_specs,
        out_specs=out_specs,
        out_shape=out_shape,
        scratch_shapes=list(scratch_shapes),
        compiler_params=_params(2),
        name=name,
    )(x, gain.reshape(1, D), mod4, mod4, *w_list, *extra)


def _outproj_kernel(o_ref, w_ref, x_ref, gate_ref, xo_ref):
    y = _dot(o_ref[...], w_ref[...])
    xo_ref[...] = x_ref[...] + gate_ref[...] * y


def _outproj_call(o, w, x, mod4, layer):
    B, T, D = x.shape
    W = o.shape[-1]
    tm = WIDE_ROW_TILE
    return pl.pallas_call(
        _outproj_kernel,
        grid=(B, T // tm),
        in_specs=[pl.BlockSpec((None, tm, W), lambda b, i: (b, i, 0)),
                  pl.BlockSpec((W, D), lambda b, i: (0, 0)),
                  pl.BlockSpec((None, tm, D), lambda b, i: (b, i, 0)),
                  pl.BlockSpec((None, None, 1, D), lambda b, i: (layer, b, 0, 2))],
        out_specs=pl.BlockSpec((None, tm, D), lambda b, i: (b, i, 0)),
        out_shape=jax.ShapeDtypeStruct((B, T, D), F32),
        compiler_params=_params(2),
        name="outproj_residual",
    )(o, w, x, mod4)


GDN_HB = GDN_HEADS
GDN_TB = 512
GDN_PREP_CHUNKS = 2


def _split2(x):
    hi = x.astype(BF16)
    return hi, (x - hi.astype(F32)).astype(BF16)


def _block_diag_pair(p):
    left = lax.broadcasted_iota(jnp.int32, p.shape, 1) < p.shape[0]
    zero = jnp.zeros_like(p)
    return jnp.concatenate([jnp.where(left, p, zero), jnp.where(left, zero, p)], axis=0)


def _dot_split_pairs(lhs_parts, p_parts):
    n = p_parts[0].shape[0]
    m = len(lhs_parts)
    his, los = zip(*lhs_parts)
    r1 = _dot(jnp.concatenate(his + los, axis=0), _block_diag_pair(p_parts[0]))
    r2 = _dot(jnp.concatenate(his, axis=0), _block_diag_pair(p_parts[1]))
    return [r1[i * n:(i + 1) * n] + r1[(m + i) * n:(m + i + 1) * n] + r2[i * n:(i + 1) * n]
            for i in range(m)]


def _unit_lower_inverses(Ls):
    n = Ls[0].shape[0]
    pairs = [jnp.concatenate([a, b], axis=1) for a, b in zip(Ls[0::2], Ls[1::2])]
    ri = lax.broadcasted_iota(jnp.int32, (n, 2 * n), 0)
    ci = lax.broadcasted_iota(jnp.int32, (n, 2 * n), 1) & (n - 1)
    eye = jnp.where(ri == ci, 1.0, 0.0)
    Xs = [eye - L for L in pairs]
    Ps = []
    for L in pairs:
        parts = _split2(L)
        Ps.append(_dot_split_pairs([parts], parts)[0])
    k = 4
    while k < n:
        outs = []
        for X, P in zip(Xs, Ps):
            p_parts = _split2(P)
            outs.append(_dot_split_pairs([_split2(X), p_parts], p_parts))
        Xs = [X + o[0] for X, o in zip(Xs, outs)]
        Ps = [o[1] for o in outs]
        k *= 2
    out = []
    for X, P in zip(Xs, Ps):
        T = X + _dot_split_pairs([_split2(X)], _split2(P))[0]
        out += [T[:, :n], T[:, n:]]
    return out


def _gdn_kernel(q_ref, k_ref, v_ref, z_ref, bg_ref, og_ref, o_ref,
                state_s, u_s, wq_s, ka_s, gl_s):
    tb = pl.program_id(2)
    TB = q_ref.shape[0]
    HB, Dh, C = GDN_HB, GDN_HEAD_DIM, CHUNK
    NC = TB // C

    @pl.when(tb == 0)
    def _():
        state_s[...] = jnp.zeros_like(state_s)

    ri = lax.broadcasted_iota(jnp.int32, (C, C), 0)
    ci = lax.broadcasted_iota(jnp.int32, (C, C), 1)
    tril = ri >= ci
    strict = ri > ci
    eye = ri == ci

    def to_row(col):
        return jnp.sum(jnp.where(eye, jnp.broadcast_to(col, (C, C)), 0.0), axis=0, keepdims=True)

    def prepare(i):
        chains = [(h, i * GDN_PREP_CHUNKS + cc) for cc in range(GDN_PREP_CHUNKS) for h in range(HB)]
        staged = []
        for h, c in chains:
            r0 = c * C if isinstance(c, int) else pl.multiple_of(c * C, C)
            q = q_ref[pl.ds(r0, C), h * Dh:(h + 1) * Dh]
            k = k_ref[pl.ds(r0, C), h * Dh:(h + 1) * Dh]
            v = v_ref[pl.ds(r0, C), h * Dh:(h + 1) * Dh]
            beta = bg_ref[pl.ds(r0, C), h:h + 1]
            g_row = to_row(bg_ref[pl.ds(r0, C), GDN_HEADS + h:GDN_HEADS + h + 1])
            gc_col = jnp.sum(jnp.where(tril, jnp.broadcast_to(g_row, (C, C)), 0.0),
                             axis=1, keepdims=True)
            gc_row = to_row(gc_col)
            g_last = gc_col[C - 1:C, :]
            decay = jnp.where(tril, jnp.exp(jnp.where(tril, gc_col - gc_row, 0.0)), 0.0)
            kb = k * beta
            egc = jnp.exp(gc_col)
            kq = _dot_nt(jnp.concatenate([kb, q], axis=0).astype(BF16), k.astype(BF16))
            L = jnp.where(strict, kq[:C] * decay, 0.0)
            attn = jnp.where(tril, kq[C:] * decay, 0.0)
            rhs = jnp.concatenate([v * beta, kb * egc], axis=1).astype(BF16)
            k_dec = k * jnp.exp(g_last - gc_col)
            wq_s[h, c, C:, :] = (q * egc).astype(BF16)
            ka_s[h, c, :Dh, :] = k_dec.T.astype(BF16)
            ka_s[h, c, Dh:, :] = attn.astype(BF16)
            gl_s[h, c] = jnp.broadcast_to(jnp.exp(g_last), (8, Dh))
            staged.append((h, c, r0, L, rhs))
        tinvs = _unit_lower_inverses([s[3] for s in staged])
        for (h, c, r0, _, rhs), tinv in zip(staged, tinvs):
            uw = _dot(tinv.astype(BF16), rhs)
            u_s[h, pl.ds(r0, C), :] = uw[:, :Dh]
            wq_s[h, c, :C, :] = uw[:, Dh:].astype(BF16)

    def recur(i):
        for cc in range(GDN_PREP_CHUNKS):
            c = i * GDN_PREP_CHUNKS + cc
            r0 = c * C if isinstance(c, int) else pl.multiple_of(c * C, C)
            for h in range(HB):
                S = state_s[h]
                r = _dot(wq_s[h, c], S.astype(BF16))
                v_new = u_s[h, pl.ds(r0, C), :] - r[:C]
                r2 = _dot(ka_s[h, c], v_new.astype(BF16))
                state_s[h] = S * gl_s[h, c][0:1, :] + r2[:Dh]
                o = r[C:] + r2[Dh:]
                z = z_ref[pl.ds(r0, C), h * Dh:(h + 1) * Dh]
                on = o * lax.rsqrt(jnp.mean(o * o, axis=-1, keepdims=True) + EPS) * og_ref[...]
                o_ref[pl.ds(r0, C), h * Dh:(h + 1) * Dh] = (on * _silu(z)).astype(o_ref.dtype)

    n_groups = NC // GDN_PREP_CHUNKS
    prepare(0)

    def body(i, carry):
        recur(i)
        prepare(i + 1)
        return carry

    lax.fori_loop(0, n_groups - 1, body, 0)
    recur(n_groups - 1)


def _gdn_call(proj, bg, o_gain):
    B, T, _ = proj.shape
    H, Dh, HB, TB, C = GDN_HEADS, GDN_HEAD_DIM, GDN_HB, GDN_TB, CHUNK
    G = H // HB
    Wb = HB * Dh
    NC = TB // C
    col = lambda off: pl.BlockSpec((None, TB, Wb), lambda b, g, t: (b, t, off + g))
    return pl.pallas_call(
        _gdn_kernel,
        grid=(B, G, T // TB),
        in_specs=[col(0), col(G), col(2 * G), col(3 * G),
                  pl.BlockSpec((None, TB, LANES), lambda b, g, t: (b, t, 0)),
                  pl.BlockSpec((1, Dh), lambda b, g, t: (0, 0))],
        out_specs=pl.BlockSpec((None, TB, Wb), lambda b, g, t: (b, t, g)),
        out_shape=jax.ShapeDtypeStruct((B, T, H * Dh), BF16),
        scratch_shapes=[pltpu.VMEM((HB, Dh, Dh), F32),
                        pltpu.VMEM((HB, TB, Dh), F32),
                        pltpu.VMEM((HB, NC, 2 * C, Dh), BF16),
                        pltpu.VMEM((HB, NC, Dh + C, C), BF16),
                        pltpu.VMEM((HB, NC, 8, Dh), F32)],
        compiler_params=_params(3),
        name="gated_delta",
    )(proj, proj, proj, proj, bg, o_gain.reshape(1, Dh))


SB_Q_PER_STEP = 8
SB_WINDOW = 3 * Q_BLOCK
LOG2_E = 1.4426950408889634
LOG2_TINY_F32 = -126.0
MASKED_LOG2 = -1e30


def _sb_log2_terms(l2):
    neg, pos = jnp.minimum(l2, 0.0), jnp.maximum(l2, 0.0)
    t2 = jnp.log2(1.0 + jnp.exp2(neg - pos))
    return neg - t2, pos + t2


def _suffix_sums(drop, later):
    r = drop.shape[0]
    s = _dot(jnp.concatenate(_split2(drop), axis=0), later)
    return s[:r] + s[r:]


def _sb_kernel(q_ref, k_ref, v_ref, z_ref, qg_ref, later_ref, o_ref):
    step = pl.program_id(2)
    QB, Dh, WK = Q_BLOCK, SB_HEAD_DIM, SB_WINDOW
    NW = WK // QB
    WA = WK - QB
    R = 2 * QB
    diag = (lax.broadcasted_iota(jnp.int32, (R, WK), 1)
            - (lax.broadcasted_iota(jnp.int32, (R, WK), 0) & (QB - 1)))
    low = lax.broadcasted_iota(jnp.int32, (QB, LANES), 1) < Dh
    first_q = step * SB_Q_PER_STEP

    pre, qms = [], []
    for qq in range(SB_Q_PER_STEP):
        qi = first_q + qq
        q2 = q_ref[qq * QB:(qq + 1) * QB, :]
        sq = q2 * q2
        s_lo = jnp.sum(jnp.where(low, sq, 0.0), axis=-1, keepdims=True)
        s_hi = jnp.sum(jnp.where(low, 0.0, sq), axis=-1, keepdims=True)
        ms = jnp.where(low, s_lo, s_hi) * (1.0 / Dh)
        qn = q2 * lax.rsqrt(ms + EPS) * qg_ref[...] * (LOG2_E * Dh ** -0.5)
        qm2 = jnp.concatenate([jnp.where(low, qn, 0.0), jnp.where(low, 0.0, qn)], axis=0).astype(BF16)
        qms.append(qm2)
        w0 = jnp.maximum(qi - (NW - 1), 0)
        k0 = pl.multiple_of(w0 * QB, QB)
        pre.append((k0, qi - w0, _dot_nt(qm2, k_ref[pl.ds(k0, WK), :])))

    in_diag = diag[:, WA:] < (NW - 1) * QB
    diag_penalty = jnp.where(in_diag, 0.0, -MASKED_LOG2)
    diag_keep = jnp.where(in_diag, 1.0, 0.0)
    mid = []
    for qq, (k0, dblk, l2) in enumerate(pre):
        log2_sig, drop = _sb_log2_terms(l2)
        if qq >= NW - 1:
            sig_a, drop_a = log2_sig[:, :WA], drop[:, :WA]
            sig_b = log2_sig[:, WA:] - diag_penalty
            drop_b = drop[:, WA:] * diag_keep
        else:
            mask = diag < dblk * QB
            log2_sig = jnp.where(mask, log2_sig, MASKED_LOG2)
            drop = jnp.where(mask, drop, 0.0)
            sig_a, sig_b, drop_a, drop_b = log2_sig[:, :WA], log2_sig[:, WA:], drop[:, :WA], drop[:, WA:]
        mid.append((k0, sig_a, sig_b, _suffix_sums(drop_a, later_ref[...]),
                    _suffix_sums(drop_b, later_ref[:QB, :QB]),
                    jnp.sum(drop_a, axis=-1, keepdims=True), jnp.sum(drop_b, axis=-1, keepdims=True)))

    accs, runs = [], []
    for k0, sig_a, sig_b, after_a, after_b, tot_a, tot_b in mid:
        w = jnp.exp2(jnp.concatenate([sig_a - (after_a + tot_b), sig_b - after_b], axis=1)).astype(BF16)
        accs.append(_dot(w, v_ref[pl.ds(k0, WK), :]))
        runs.append(tot_a + tot_b)

    def unfinished(run):
        return jnp.min(run) <= -LOG2_TINY_F32

    def earlier_blocks():
        out = []
        for qq in range(SB_Q_PER_STEP):
            def cond(c):
                j, _, run = c
                return jnp.logical_and(j >= 0, unfinished(run))

            def body(c, qm2=qms[qq]):
                j, acc, run = c
                r0 = pl.multiple_of(j * QB, QB)
                log2_sig, drop = _sb_log2_terms(_dot_nt(qm2, k_ref[pl.ds(r0, QB), :]))
                w = jnp.exp2(log2_sig - (_suffix_sums(drop, later_ref[:QB, :QB]) + run)).astype(BF16)
                acc = acc + _dot(w, v_ref[pl.ds(r0, QB), :])
                return j - 1, acc, run + jnp.sum(drop, axis=-1, keepdims=True)

            out.append(lax.while_loop(cond, body, (first_q + qq - NW, accs[qq], runs[qq]))[1])
        return out

    slowest = functools.reduce(jnp.minimum, runs)
    any_left = jnp.logical_and(first_q + SB_Q_PER_STEP - 1 >= NW, unfinished(slowest))
    accs = lax.cond(any_left, earlier_blocks, lambda: accs)
    for qq, acc in enumerate(accs):
        rows = slice(qq * QB, (qq + 1) * QB)
        o_ref[rows, :] = (jnp.where(low, acc[:QB], acc[QB:]) * _silu(z_ref[rows, :])).astype(o_ref.dtype)


def _sb_call(proj, kv, q_gain):
    B, T, W2 = proj.shape
    W = W2 // 2
    nb = W // LANES
    tq = SB_Q_PER_STEP * Q_BLOCK
    wa = SB_WINDOW - Q_BLOCK
    qg = jnp.tile(q_gain, LANES // SB_HEAD_DIM).reshape(1, LANES)
    later = jnp.tri(wa, k=-1, dtype=BF16)
    return pl.pallas_call(
        _sb_kernel,
        grid=(B, nb, T // tq),
        in_specs=[pl.BlockSpec((None, tq, LANES), lambda b, p, i: (b, i, p)),
                  pl.BlockSpec((None, T, LANES), lambda b, p, i: (b, 0, p)),
                  pl.BlockSpec((None, T, LANES), lambda b, p, i: (b, 0, nb + p)),
                  pl.BlockSpec((None, tq, LANES), lambda b, p, i: (b, i, nb + p)),
                  pl.BlockSpec((1, LANES), lambda b, p, i: (0, 0)),
                  pl.BlockSpec((wa, wa), lambda b, p, i: (0, 0))],
        out_specs=pl.BlockSpec((None, tq, LANES), lambda b, p, i: (b, i, p)),
        out_shape=jax.ShapeDtypeStruct((B, T, W), BF16),
        compiler_params=_params(3),
        name="stick_breaking",
    )(proj, kv, kv, proj, qg, later)


def kernel(x, c, norm_g, w_ada, b_ada, w_in_a, conv_w_a, a_log_a, dt_bias_a, o_gain_a, w_out_a,
           kv_norm_g, w_ada_kv, b_ada_kv, w_kv, k_gain, w_in_b, q_gain_b, w_out_b):
    B, T, D = x.shape
    depth = w_ada.shape[0]
    n_a = w_in_a.shape[0]
    gw = GDN_HEADS * GDN_HEAD_DIM

    mod4 = _mod_call(c, w_ada, b_ada).reshape(depth, B, 1, 3 * D)
    modkv4 = _mod_call(c, w_ada_kv[None], b_ada_kv[None]).reshape(1, B, 1, 2 * D)

    kv = None
    for layer in range(depth):
        if layer < n_a:
            i = layer
            w_main = w_in_a[i, :, :4 * gw].astype(BF16)
            w_tail = jnp.pad(w_in_a[i, :, 4 * gw:], ((0, 0), (0, LANES - 2 * GDN_HEADS))).astype(BF16)
            gate_row = lambda vec: jnp.zeros((1, LANES), F32).at[0, GDN_HEADS:2 * GDN_HEADS].set(vec)
            proj, bg = _norm_proj_call(
                _gdn_in_proj_kernel, x, norm_g[layer], mod4, layer, [w_main, w_tail],
                extra=(conv_w_a[i], gate_row(a_log_a[i]), gate_row(dt_bias_a[i])), name="gdn_in_proj",
                scratch_shapes=[pltpu.VMEM((HALO + ROW_TILE, 3 * gw), F32)] * 2
                + [pltpu.VMEM((ROW_TILE, N_CHUNK), F32)])
            o = _gdn_call(proj, bg, o_gain_a[i])
            x = _outproj_call(o, w_out_a[i].astype(BF16), x, mod4, layer)
        else:
            if kv is None:
                kg = jnp.tile(k_gain, LANES // SB_HEAD_DIM).reshape(1, LANES)
                (kv,) = _norm_proj_call(_norm_kv_kernel, x, kv_norm_g, modkv4, 0,
                                        [w_kv.astype(BF16)], extra=(kg,), name="shared_kv",
                                        out_dtype=BF16, tm=WIDE_ROW_TILE)
            i = layer - n_a
            (proj,) = _norm_proj_call(_norm_proj_kernel, x, norm_g[layer], mod4, layer,
                                      [w_in_b[i].astype(BF16)], name="sb_in_proj", tm=WIDE_ROW_TILE)
            o = _sb_call(proj, kv, q_gain_b[i])
            x = _outproj_call(o, w_out_b[i].astype(BF16), x, mod4, layer)
    return x
```

```python
import functools

import jax
import jax.numpy as jnp
from jax import lax
from jax.experimental import pallas as pl
from jax.experimental.pallas import tpu as pltpu

F32 = jnp.float32
BF16 = jnp.bfloat16
HI = lax.Precision.HIGHEST

EPS = 1e-6
CHUNK = 64
GDN_HEADS = 8
GDN_HEAD_DIM = 128
CONV_WIDTH = 4
SB_HEADS = 16
SB_HEAD_DIM = 64
Q_BLOCK = 128
LANES = 128
VMEM_LIMIT = 48 * 1024 * 1024
ROW_TILE = 256
WIDE_ROW_TILE = 1024
N_CHUNK = 512
HALO = 8


def _params(n_axes):
    return pltpu.CompilerParams(dimension_semantics=("arbitrary",) * n_axes,
                                vmem_limit_bytes=VMEM_LIMIT)


def _silu(x):
    return x * jax.nn.sigmoid(x)


def _dot(a, b, precision=None):
    return jnp.dot(a, b, precision=precision, preferred_element_type=F32)


def _dot_nt(a, b, precision=None):
    return lax.dot_general(a, b, (((1,), (1,)), ((), ())), precision=precision,
                           preferred_element_type=F32)


def _mod_kernel(c_ref, w_ref, b_ref, o_ref):
    ca = _silu(c_ref[...])
    o_ref[...] = _dot(ca, w_ref[...], HI) + b_ref[...]


def _mod_call(c, w, b):
    L, D, N = w.shape
    B = c.shape[0]
    tn = 1024
    return pl.pallas_call(
        _mod_kernel,
        grid=(L, N // tn),
        in_specs=[pl.BlockSpec((B, D), lambda l, j: (0, 0)),
                  pl.BlockSpec((None, D, tn), lambda l, j: (l, 0, j)),
                  pl.BlockSpec((None, 1, tn), lambda l, j: (l, 0, j))],
        out_specs=pl.BlockSpec((None, B, tn), lambda l, j: (l, 0, j)),
        out_shape=jax.ShapeDtypeStruct((L, B, N), F32),
        compiler_params=_params(2),
        name="adaln_mod",
    )(c, w, b.reshape(L, 1, N))


def _modnorm(x_ref, g_ref, sh_ref, sc_ref):
    x = x_ref[...]
    n = x * lax.rsqrt(jnp.mean(x * x, axis=-1, keepdims=True) + EPS) * g_ref[...]
    return (n * (1.0 + sc_ref[...]) + sh_ref[...]).astype(BF16)


def _proj_cols(hb, w_ref, o_ref):
    n = w_ref.shape[1]
    for j in range(0, n, N_CHUNK):
        o_ref[:, j:j + N_CHUNK] = _dot(hb, w_ref[:, j:j + N_CHUNK])


def _norm_proj_kernel(x_ref, g_ref, sh_ref, sc_ref, w_ref, o_ref):
    _proj_cols(_modnorm(x_ref, g_ref, sh_ref, sc_ref), w_ref, o_ref)


def _gdn_in_proj_kernel(x_ref, g_ref, sh_ref, sc_ref, w_ref, wt_ref, cw_ref, alog_ref, dtb_ref,
                        o_ref, bg_ref, xe_s, ve_s, prev_s):
    assert CONV_WIDTH == 4
    tm = x_ref.shape[0]
    Dh = GDN_HEAD_DIM
    w3 = cw_ref.shape[1]
    wd = w3 // 3

    @pl.when(pl.program_id(1) == 0)
    def _():
        xe_s[:HALO, :] = jnp.zeros((HALO, w3), F32)
        ve_s[:HALO, :] = jnp.zeros((HALO, w3), F32)

    hb = _modnorm(x_ref, g_ref, sh_ref, sc_ref)
    for j in range(0, w3, N_CHUNK):
        cols = slice(j, j + N_CHUNK)
        raw = _dot(hb, w_ref[:, cols])
        xe_s[HALO:, cols] = raw
        prev_s[...] = xe_s[HALO - 1:HALO - 1 + tm, cols]
        v = raw * cw_ref[1:2, cols] + prev_s[...] * cw_ref[0:1, cols]
        ve_s[HALO:, cols] = v
        acc = (raw * cw_ref[3:4, cols] + prev_s[...] * cw_ref[2:3, cols]
               + ve_s[HALO - 2:HALO - 2 + tm, cols])
        xe_s[:HALO, cols] = raw[tm - HALO:]
        ve_s[:HALO, cols] = v[tm - HALO:]
        y = _silu(acc)
        if j < 2 * wd:
            scale = Dh ** -0.5 if j < wd else 1.0
            for h0 in range(0, N_CHUNK, Dh):
                yh = y[:, h0:h0 + Dh]
                inv = lax.rsqrt(jnp.sum(yh * yh, axis=-1, keepdims=True) + EPS) * scale
                o_ref[:, j + h0:j + h0 + Dh] = yh * inv
        else:
            o_ref[:, cols] = y
    for j in range(w3, w_ref.shape[1], N_CHUNK):
        o_ref[:, j:j + N_CHUNK] = _dot(hb, w_ref[:, j:j + N_CHUNK])
    t = _dot(hb, wt_ref[...])
    a = t + dtb_ref[...]
    g = -jnp.exp(alog_ref[...]) * (jnp.maximum(a, 0.0) + jnp.log1p(jnp.exp(-jnp.abs(a))))
    lane = lax.broadcasted_iota(jnp.int32, t.shape, 1)
    bg_ref[...] = jnp.where(lane < GDN_HEADS, jax.nn.sigmoid(t), g)


def _norm_kv_kernel(x_ref, g_ref, sh_ref, sc_ref, w_ref, kg_ref, o_ref):
    hb = _modnorm(x_ref, g_ref, sh_ref, sc_ref)
    n = w_ref.shape[1]
    half = n // 2
    tm = x_ref.shape[0]
    low = lax.broadcasted_iota(jnp.int32, (tm, LANES), 1) < SB_HEAD_DIM
    for j in range(0, half, LANES):
        kk = _dot(hb, w_ref[:, j:j + LANES])
        sq = kk * kk
        s_lo = jnp.sum(jnp.where(low, sq, 0.0), axis=-1, keepdims=True)
        s_hi = jnp.sum(jnp.where(low, 0.0, sq), axis=-1, keepdims=True)
        ms = jnp.where(low, s_lo, s_hi) * (1.0 / SB_HEAD_DIM)
        o_ref[:, j:j + LANES] = (kk * lax.rsqrt(ms + EPS) * kg_ref[...]).astype(o_ref.dtype)
    for j in range(half, n, N_CHUNK):
        o_ref[:, j:j + N_CHUNK] = _dot(hb, w_ref[:, j:j + N_CHUNK]).astype(o_ref.dtype)


def _norm_proj_call(kernel_fn, x, gain, mod4, layer, w_list, extra=(), name="norm_proj",
                    out_dtype=F32, scratch_shapes=(), tm=ROW_TILE):
    B, T, D = x.shape
    in_specs = [pl.BlockSpec((None, tm, D), lambda b, i: (b, i, 0)),
                pl.BlockSpec((1, D), lambda b, i: (0, 0)),
                pl.BlockSpec((None, None, 1, D), lambda b, i: (layer, b, 0, 0)),
                pl.BlockSpec((None, None, 1, D), lambda b, i: (layer, b, 0, 1))]
    out_specs, out_shape = [], []
    for w in w_list:
        n = w.shape[1]
        in_specs.append(pl.BlockSpec((D, n), lambda b, i: (0, 0)))
        out_specs.append(pl.BlockSpec((None, tm, n), lambda b, i: (b, i, 0)))
        out_shape.append(jax.ShapeDtypeStruct((B, T, n), out_dtype))
    for e in extra:
        in_specs.append(pl.BlockSpec(e.shape, lambda b, i: (0, 0)))
    return pl.pallas_call(
        kernel_fn,
        grid=(B, T // tm),
        in_specs=in_specs,
        out_specs=out_specs,
        out_shape=out_shape,
        scratch_shapes=list(scratch_shapes),
        compiler_params=_params(2),
        name=name,
    )(x, gain.reshape(1, D), mod4, mod4, *w_list, *extra)


def _outproj_kernel(o_ref, w_ref, x_ref, gate_ref, xo_ref):
    y = _dot(o_ref[...], w_ref[...])
    xo_ref[...] = x_ref[...] + gate_ref[...] * y


def _outproj_call(o, w, x, mod4, layer):
    B, T, D = x.shape
    W = o.shape[-1]
    tm = WIDE_ROW_TILE
    return pl.pallas_call(
        _outproj_kernel,
        grid=(B, T // tm),
        in_specs=[pl.BlockSpec((None, tm, W), lambda b, i: (b, i, 0)),
                  pl.BlockSpec((W, D), lambda b, i: (0, 0)),
                  pl.BlockSpec((None, tm, D), lambda b, i: (b, i, 0)),
                  pl.BlockSpec((None, None, 1, D), lambda b, i: (layer, b, 0, 2))],
        out_specs=pl.BlockSpec((None, tm, D), lambda b, i: (b, i, 0)),
        out_shape=jax.ShapeDtypeStruct((B, T, D), F32),
        compiler_params=_params(2),
        name="outproj_residual",
    )(o, w, x, mod4)


GDN_HB = GDN_HEADS
GDN_TB = 512
GDN_PREP_CHUNKS = 2


def _split2(x):
    hi = x.astype(BF16)
    return hi, (x - hi.astype(F32)).astype(BF16)


def _block_diag_pair(p):
    left = lax.broadcasted_iota(jnp.int32, p.shape, 1) < p.shape[0]
    zero = jnp.zeros_like(p)
    return jnp.concatenate([jnp.where(left, p, zero), jnp.where(left, zero, p)], axis=0)


def _dot_split_pairs(lhs_parts, p_parts):
    n = p_parts[0].shape[0]
    m = len(lhs_parts)
    his, los = zip(*lhs_parts)
    r1 = _dot(jnp.concatenate(his + los, axis=0), _block_diag_pair(p_parts[0]))
    r2 = _dot(jnp.concatenate(his, axis=0), _block_diag_pair(p_parts[1]))
    return [r1[i * n:(i + 1) * n] + r1[(m + i) * n:(m + i + 1) * n] + r2[i * n:(i + 1) * n]
            for i in range(m)]


def _unit_lower_inverses(Ls):
    n = Ls[0].shape[0]
    pairs = [jnp.concatenate([a, b], axis=1) for a, b in zip(Ls[0::2], Ls[1::2])]
    ri = lax.broadcasted_iota(jnp.int32, (n, 2 * n), 0)
    ci = lax.broadcasted_iota(jnp.int32, (n, 2 * n), 1) & (n - 1)
    eye = jnp.where(ri == ci, 1.0, 0.0)
    Xs = [eye - L for L in pairs]
    Ps = []
    for L in pairs:
        parts = _split2(L)
        Ps.append(_dot_split_pairs([parts], parts)[0])
    k = 4
    while k < n:
        outs = []
        for X, P in zip(Xs, Ps):
            p_parts = _split2(P)
            outs.append(_dot_split_pairs([_split2(X), p_parts], p_parts))
        Xs = [X + o[0] for X, o in zip(Xs, outs)]
        Ps = [o[1] for o in outs]
        k *= 2
    out = []
    for X, P in zip(Xs, Ps):
        T = X + _dot_split_pairs([_split2(X)], _split2(P))[0]
        out += [T[:, :n], T[:, n:]]
    return out


def _gdn_kernel(q_ref, k_ref, v_ref, z_ref, bg_ref, og_ref, o_ref,
                state_s, u_s, wq_s, ka_s, gl_s):
    tb = pl.program_id(2)
    TB = q_ref.shape[0]
    HB, Dh, C = GDN_HB, GDN_HEAD_DIM, CHUNK
    NC = TB // C

    @pl.when(tb == 0)
    def _():
        state_s[...] = jnp.zeros_like(state_s)

    ri = lax.broadcasted_iota(jnp.int32, (C, C), 0)
    ci = lax.broadcasted_iota(jnp.int32, (C, C), 1)
    tril = ri >= ci
    strict = ri > ci
    eye = ri == ci

    def to_row(col):
        return jnp.sum(jnp.where(eye, jnp.broadcast_to(col, (C, C)), 0.0), axis=0, keepdims=True)

    def prepare(i):
        chains = [(h, i * GDN_PREP_CHUNKS + cc) for cc in range(GDN_PREP_CHUNKS) for h in range(HB)]
        staged = []
        for h, c in chains:
            r0 = c * C if isinstance(c, int) else pl.multiple_of(c * C, C)
            q = q_ref[pl.ds(r0, C), h * Dh:(h + 1) * Dh]
            k = k_ref[pl.ds(r0, C), h * Dh:(h + 1) * Dh]
            v = v_ref[pl.ds(r0, C), h * Dh:(h + 1) * Dh]
            beta = bg_ref[pl.ds(r0, C), h:h + 1]
            g_row = to_row(bg_ref[pl.ds(r0, C), GDN_HEADS + h:GDN_HEADS + h + 1])
            gc_col = jnp.sum(jnp.where(tril, jnp.broadcast_to(g_row, (C, C)), 0.0),
                             axis=1, keepdims=True)
            gc_row = to_row(gc_col)
            g_last = gc_col[C - 1:C, :]
            decay = jnp.where(tril, jnp.exp(jnp.where(tril, gc_col - gc_row, 0.0)), 0.0)
            kb = k * beta
            egc = jnp.exp(gc_col)
            kq = _dot_nt(jnp.concatenate([kb, q], axis=0).astype(BF16), k.astype(BF16))
            L = jnp.where(strict, kq[:C] * decay, 0.0)
            attn = jnp.where(tril, kq[C:] * decay, 0.0)
            rhs = jnp.concatenate([v * beta, kb * egc], axis=1).astype(BF16)
            k_dec = k * jnp.exp(g_last - gc_col)
            wq_s[h, c, C:, :] = (q * egc).astype(BF16)
            ka_s[h, c, :Dh, :] = k_dec.T.astype(BF16)
            ka_s[h, c, Dh:, :] = attn.astype(BF16)
            gl_s[h, c] = jnp.broadcast_to(jnp.exp(g_last), (8, Dh))
            staged.append((h, c, r0, L, rhs))
        tinvs = _unit_lower_inverses([s[3] for s in staged])
        for (h, c, r0, _, rhs), tinv in zip(staged, tinvs):
            uw = _dot(tinv.astype(BF16), rhs)
            u_s[h, pl.ds(r0, C), :] = uw[:, :Dh]
            wq_s[h, c, :C, :] = uw[:, Dh:].astype(BF16)

    def recur(i):
        for cc in range(GDN_PREP_CHUNKS):
            c = i * GDN_PREP_CHUNKS + cc
            r0 = c * C if isinstance(c, int) else pl.multiple_of(c * C, C)
            for h in range(HB):
                S = state_s[h]
                r = _dot(wq_s[h, c], S.astype(BF16))
                v_new = u_s[h, pl.ds(r0, C), :] - r[:C]
                r2 = _dot(ka_s[h, c], v_new.astype(BF16))
                state_s[h] = S * gl_s[h, c][0:1, :] + r2[:Dh]
                o = r[C:] + r2[Dh:]
                z = z_ref[pl.ds(r0, C), h * Dh:(h + 1) * Dh]
                on = o * lax.rsqrt(jnp.mean(o * o, axis=-1, keepdims=True) + EPS) * og_ref[...]
                o_ref[pl.ds(r0, C), h * Dh:(h + 1) * Dh] = (on * _silu(z)).astype(o_ref.dtype)

    n_groups = NC // GDN_PREP_CHUNKS
    prepare(0)

    def body(i, carry):
        recur(i)
        prepare(i + 1)
        return carry

    lax.fori_loop(0, n_groups - 1, body, 0)
    recur(n_groups - 1)


def _gdn_call(proj, bg, o_gain):
    B, T, _ = proj.shape
    H, Dh, HB, TB, C = GDN_HEADS, GDN_HEAD_DIM, GDN_HB, GDN_TB, CHUNK
    G = H // HB
    Wb = HB * Dh
    NC = TB // C
    col = lambda off: pl.BlockSpec((None, TB, Wb), lambda b, g, t: (b, t, off + g))
    return pl.pallas_call(
        _gdn_kernel,
        grid=(B, G, T // TB),
        in_specs=[col(0), col(G), col(2 * G), col(3 * G),
                  pl.BlockSpec((None, TB, LANES), lambda b, g, t: (b, t, 0)),
                  pl.BlockSpec((1, Dh), lambda b, g, t: (0, 0))],
        out_specs=pl.BlockSpec((None, TB, Wb), lambda b, g, t: (b, t, g)),
        out_shape=jax.ShapeDtypeStruct((B, T, H * Dh), BF16),
        scratch_shapes=[pltpu.VMEM((HB, Dh, Dh), F32),
                        pltpu.VMEM((HB, TB, Dh), F32),
                        pltpu.VMEM((HB, NC, 2 * C, Dh), BF16),
                        pltpu.VMEM((HB, NC, Dh + C, C), BF16),
                        pltpu.VMEM((HB, NC, 8, Dh), F32)],
        compiler_params=_params(3),
        name="gated_delta",
    )(proj, proj, proj, proj, bg, o_gain.reshape(1, Dh))


SB_Q_PER_STEP = 16
SB_WINDOW = 3 * Q_BLOCK
LOG2_E = 1.4426950408889634
LOG2_TINY_F32 = -126.0
MASKED_LOG2 = -1e30


def _sb_log2_terms(l2):
    neg, pos = jnp.minimum(l2, 0.0), jnp.maximum(l2, 0.0)
    t2 = jnp.log2(1.0 + jnp.exp2(neg - pos))
    return neg - t2, pos + t2


def _suffix_sums(drop, later):
    r = drop.shape[0]
    s = _dot(jnp.concatenate(_split2(drop), axis=0), later)
    return s[:r] + s[r:]


def _sb_kernel(q_ref, k_ref, v_ref, z_ref, qg_ref, later_ref, o_ref):
    step = pl.program_id(2)
    QB, Dh, WK = Q_BLOCK, SB_HEAD_DIM, SB_WINDOW
    NW = WK // QB
    WA = WK - QB
    R = 2 * QB
    diag = (lax.broadcasted_iota(jnp.int32, (R, WK), 1)
            - (lax.broadcasted_iota(jnp.int32, (R, WK), 0) & (QB - 1)))
    low = lax.broadcasted_iota(jnp.int32, (QB, LANES), 1) < Dh
    first_q = step * SB_Q_PER_STEP

    pre, qms = [], []
    for qq in range(SB_Q_PER_STEP):
        qi = first_q + qq
        q2 = q_ref[qq * QB:(qq + 1) * QB, :]
        sq = q2 * q2
        s_lo = jnp.sum(jnp.where(low, sq, 0.0), axis=-1, keepdims=True)
        s_hi = jnp.sum(jnp.where(low, 0.0, sq), axis=-1, keepdims=True)
        ms = jnp.where(low, s_lo, s_hi) * (1.0 / Dh)
        qn = q2 * lax.rsqrt(ms + EPS) * qg_ref[...] * (LOG2_E * Dh ** -0.5)
        qm2 = jnp.concatenate([jnp.where(low, qn, 0.0), jnp.where(low, 0.0, qn)], axis=0).astype(BF16)
        qms.append(qm2)
        w0 = jnp.maximum(qi - (NW - 1), 0)
        k0 = pl.multiple_of(w0 * QB, QB)
        pre.append((k0, qi - w0, _dot_nt(qm2, k_ref[pl.ds(k0, WK), :])))

    in_diag = diag[:, WA:] < (NW - 1) * QB
    diag_penalty = jnp.where(in_diag, 0.0, -MASKED_LOG2)
    diag_keep = jnp.where(in_diag, 1.0, 0.0)
    mid = []
    for qq, (k0, dblk, l2) in enumerate(pre):
        log2_sig, drop = _sb_log2_terms(l2)
        if qq >= NW - 1:
            sig_a, drop_a = log2_sig[:, :WA], drop[:, :WA]
            sig_b = log2_sig[:, WA:] - diag_penalty
            drop_b = drop[:, WA:] * diag_keep
        else:
            mask = diag < dblk * QB
            log2_sig = jnp.where(mask, log2_sig, MASKED_LOG2)
            drop = jnp.where(mask, drop, 0.0)
            sig_a, sig_b, drop_a, drop_b = log2_sig[:, :WA], log2_sig[:, WA:], drop[:, :WA], drop[:, WA:]
        mid.append((k0, sig_a, sig_b, _suffix_sums(drop_a, later_ref[...]),
                    _suffix_sums(drop_b, later_ref[:QB, :QB]),
                    jnp.sum(drop_a, axis=-1, keepdims=True), jnp.sum(drop_b, axis=-1, keepdims=True)))

    accs, runs = [], []
    for k0, sig_a, sig_b, after_a, after_b, tot_a, tot_b in mid:
        w = jnp.exp2(jnp.concatenate([sig_a - (after_a + tot_b), sig_b - after_b], axis=1)).astype(BF16)
        accs.append(_dot(w, v_ref[pl.ds(k0, WK), :]))
        runs.append(tot_a + tot_b)

    def unfinished(run):
        return jnp.min(run) <= -LOG2_TINY_F32

    def earlier_blocks():
        out = []
        for qq in range(SB_Q_PER_STEP):
            def cond(c):
                j, _, run = c
                return jnp.logical_and(j >= 0, unfinished(run))

            def body(c, qm2=qms[qq]):
                j, acc, run = c
                r0 = pl.multiple_of(j * QB, QB)
                log2_sig, drop = _sb_log2_terms(_dot_nt(qm2, k_ref[pl.ds(r0, QB), :]))
                w = jnp.exp2(log2_sig - (_suffix_sums(drop, later_ref[:QB, :QB]) + run)).astype(BF16)
                acc = acc + _dot(w, v_ref[pl.ds(r0, QB), :])
                return j - 1, acc, run + jnp.sum(drop, axis=-1, keepdims=True)

            out.append(lax.while_loop(cond, body, (first_q + qq - NW, accs[qq], runs[qq]))[1])
        return out

    slowest = functools.reduce(jnp.minimum, runs)
    any_left = jnp.logical_and(first_q + SB_Q_PER_STEP - 1 >= NW, unfinished(slowest))
    accs = lax.cond(any_left, earlier_blocks, lambda: accs)
    for qq, acc in enumerate(accs):
        rows = slice(qq * QB, (qq + 1) * QB)
        o_ref[rows, :] = (jnp.where(low, acc[:QB], acc[QB:]) * _silu(z_ref[rows, :])).astype(o_ref.dtype)


def _sb_call(proj, kv, q_gain):
    B, T, W2 = proj.shape
    W = W2 // 2
    nb = W // LANES
    tq = SB_Q_PER_STEP * Q_BLOCK
    wa = SB_WINDOW - Q_BLOCK
    qg = jnp.tile(q_gain, LANES // SB_HEAD_DIM).reshape(1, LANES)
    later = jnp.tri(wa, k=-1, dtype=BF16)
    return pl.pallas_call(
        _sb_kernel,
        grid=(B, nb, T // tq),
        in_specs=[pl.BlockSpec((None, tq, LANES), lambda b, p, i: (b, i, p)),
                  pl.BlockSpec((None, T, LANES), lambda b, p, i: (b, 0, p)),
                  pl.BlockSpec((None, T, LANES), lambda b, p, i: (b, 0, nb + p)),
                  pl.BlockSpec((None, tq, LANES), lambda b, p, i: (b, i, nb + p)),
                  pl.BlockSpec((1, LANES), lambda b, p, i: (0, 0)),
                  pl.BlockSpec((wa, wa), lambda b, p, i: (0, 0))],
        out_specs=pl.BlockSpec((None, tq, LANES), lambda b, p, i: (b, i, p)),
        out_shape=jax.ShapeDtypeStruct((B, T, W), BF16),
        compiler_params=_params(3),
        name="stick_breaking",
    )(proj, kv, kv, proj, qg, later)


def kernel(x, c, norm_g, w_ada, b_ada, w_in_a, conv_w_a, a_log_a, dt_bias_a, o_gain_a, w_out_a,
           kv_norm_g, w_ada_kv, b_ada_kv, w_kv, k_gain, w_in_b, q_gain_b, w_out_b):
    B, T, D = x.shape
    depth = w_ada.shape[0]
    n_a = w_in_a.shape[0]
    gw = GDN_HEADS * GDN_HEAD_DIM

    mod4 = _mod_call(c, w_ada, b_ada).reshape(depth, B, 1, 3 * D)
    modkv4 = _mod_call(c, w_ada_kv[None], b_ada_kv[None]).reshape(1, B, 1, 2 * D)

    kv = None
    for layer in range(depth):
        if layer < n_a:
            i = layer
            w_main = w_in_a[i, :, :4 * gw].astype(BF16)
            w_tail = jnp.pad(w_in_a[i, :, 4 * gw:], ((0, 0), (0, LANES - 2 * GDN_HEADS))).astype(BF16)
            gate_row = lambda vec: jnp.zeros((1, LANES), F32).at[0, GDN_HEADS:2 * GDN_HEADS].set(vec)
            proj, bg = _norm_proj_call(
                _gdn_in_proj_kernel, x, norm_g[layer], mod4, layer, [w_main, w_tail],
                extra=(conv_w_a[i], gate_row(a_log_a[i]), gate_row(dt_bias_a[i])), name="gdn_in_proj",
                scratch_shapes=[pltpu.VMEM((HALO + ROW_TILE, 3 * gw), F32)] * 2
                + [pltpu.VMEM((ROW_TILE, N_CHUNK), F32)])
            o = _gdn_call(proj, bg, o_gain_a[i])
            x = _outproj_call(o, w_out_a[i].astype(BF16), x, mod4, layer)
        else:
            if kv is None:
                kg = jnp.tile(k_gain, LANES // SB_HEAD_DIM).reshape(1, LANES)
                (kv,) = _norm_proj_call(_norm_kv_kernel, x, kv_norm_g, modkv4, 0,
                                        [w_kv.astype(BF16)], extra=(kg,), name="shared_kv",
                                        out_dtype=BF16, tm=WIDE_ROW_TILE)
            i = layer - n_a
            (proj,) = _norm_proj_call(_norm_proj_kernel, x, norm_g[layer], mod4, layer,
                                      [w_in_b[i].astype(BF16)], name="sb_in_proj", tm=WIDE_ROW_TILE)
            o = _sb_call(proj, kv, q_gain_b[i])
            x = _outproj_call(o, w_out_b[i].astype(BF16), x, mod4, layer)
    return x
```
